```python
import math
import jax, jax.numpy as jnp
from jax import lax
import numpy as np

D_MODEL = 1024
BATCH = 2
SEQ = 8192
DEPTH = 2
DEC_BATCH = 128
DEC_SEQ = 8
PAST_LEN = 2048
PAGE_SIZE = 128

HEAD_DIM = 64
N_HEADS_A = 8
N_HEADS_B = 8
N_HEADS_C = 16
D_A = N_HEADS_A * HEAD_DIM
D_B = N_HEADS_B * HEAD_DIM
D_C = N_HEADS_C * HEAD_DIM
IN_EVEN = 3 * D_A + 3 * D_B + N_HEADS_B
IN_ODD = 3 * D_C
MOBA_BLOCK = 256
MOBA_TOPK = 3
QUERY_BLOCK = 128
DILATED_BRANCHES = ((128, 1), (512, 4), (2048, 16))
WIN_MAX = 2048
PEER_HEADS = 8
PEER_KEYS = 128
PEER_EXPERTS = PEER_KEYS * PEER_KEYS
PEER_TOPK = 16
PEER_DKEY = 256
PEER_ROW = 128
FORGET_BIAS = 2.0
RMS_EPS = 1e-6
NEG = -1e30
N_EVEN = (DEPTH + 1) // 2
N_ODD = DEPTH // 2

kernel_name = "hybrid_moba_fox_dilated_peer_step"

F32 = jnp.float32


def rmsnorm(x, g):
    xf = x.astype(F32)
    y = xf * lax.rsqrt(jnp.mean(xf * xf, axis=-1, keepdims=True) + RMS_EPS)
    return (y * g.astype(F32)).astype(x.dtype)


def alibi_slopes(n):
    return jnp.exp2(-8.0 * jnp.arange(1, n + 1, dtype=F32) / n)


def heads(t, n):
    return t.reshape(t.shape[:-1] + (n, HEAD_DIM))


def to_blocks(k, n_blocks):
    L, H, hd = k.shape
    k = jnp.pad(k, ((0, n_blocks * MOBA_BLOCK - L), (0, 0), (0, 0)))
    return k.reshape(n_blocks, MOBA_BLOCK, H, hd).transpose(2, 0, 1, 3)


def moba_chunk(q, k_blk, v_blk, k_mean, q_pos, own):
    H, NB, MB, hd = k_blk.shape
    Q = q.shape[0]
    kk = min(MOBA_TOPK, NB)
    scale = hd ** -0.5
    slopes = alibi_slopes(H)
    gate = jnp.einsum('qhd,hnd->hqn', q.astype(F32), k_mean)
    gate = jnp.where((jnp.arange(NB) < own)[None, None, :], gate, NEG)
    _, idx = lax.top_k(gate, kk)
    valid = idx < own
    h_ix = jnp.arange(H)[:, None, None]
    k_sel = k_blk[h_ix, idx]
    v_sel = v_blk[h_ix, idx].reshape(H, Q, kk * MB, hd)
    pos_sel = idx[..., None] * MB + jnp.arange(MB)
    dist_sel = (q_pos[None, :, None, None] - pos_sel).astype(F32)
    l_sel = jnp.einsum('qhd,hqkmd->hqkm', q, k_sel, preferred_element_type=F32) * scale \
        - slopes[:, None, None, None] * dist_sel
    l_sel = jnp.where(valid[..., None], l_sel, NEG).reshape(H, Q, kk * MB)
    k_own = lax.dynamic_index_in_dim(k_blk, own, axis=1, keepdims=False)
    v_own = lax.dynamic_index_in_dim(v_blk, own, axis=1, keepdims=False)
    pos_own = own * MB + jnp.arange(MB)
    dist_own = (q_pos[:, None] - pos_own[None, :]).astype(F32)
    l_own = jnp.einsum('qhd,hmd->hqm', q, k_own, preferred_element_type=F32) * scale \
        - slopes[:, None, None] * dist_own[None]
    l_own = jnp.where((dist_own >= 0)[None], l_own, NEG)
    p = jax.nn.softmax(jnp.concatenate([l_sel, l_own], axis=-1), axis=-1)
    out = jnp.einsum('hqk,hqkd->qhd', p[..., :kk * MB], v_sel.astype(F32)) \
        + jnp.einsum('hqm,hmd->qhd', p[..., kk * MB:], v_own.astype(F32))
    return out.astype(q.dtype)


def moba_prompt(q, k, v):
    B, S, H, hd = q.shape
    nb = -(-S // MOBA_BLOCK)
    kb = jax.vmap(lambda t: to_blocks(t, nb))(k)
    vb = jax.vmap(lambda t: to_blocks(t, nb))(v)
    km = jnp.mean(kb.astype(F32), axis=3)
    nq = S // QUERY_BLOCK
    qc = q.reshape(B, nq, QUERY_BLOCK, H, hd).swapaxes(0, 1)

    def step(args):
        i, qb = args
        start = i * QUERY_BLOCK
        q_pos = start + jnp.arange(QUERY_BLOCK, dtype=jnp.int32)
        own = start // MOBA_BLOCK
        return jax.vmap(lambda a, b_, c, d: moba_chunk(a, b_, c, d, q_pos, own))(qb, kb, vb, km)

    out = lax.map(step, (jnp.arange(nq, dtype=jnp.int32), qc))
    return out.swapaxes(0, 1).reshape(B, S, H, hd)


def moba_sample(q, k_all, v_all, past_len):
    DS = q.shape[1]
    L = k_all.shape[1]
    nb = -(-L // MOBA_BLOCK)
    q_pos = past_len + jnp.arange(DS, dtype=jnp.int32)
    own = past_len // MOBA_BLOCK

    def one(args):
        qs, ks, vs = args
        kb = to_blocks(ks, nb)
        vb = to_blocks(vs, nb)
        km = jnp.mean(kb.astype(F32), axis=2)
        return moba_chunk(qs, kb, vb, km, q_pos, own)

    return lax.map(one, (q, k_all, v_all))


def fox_chunk(q, k, v, cq, ck, q_pos, k_pos):
    hd = q.shape[-1]
    l = jnp.einsum('qhd,khd->hqk', q, k, preferred_element_type=F32) * (hd ** -0.5) \
        + (cq.T[:, :, None] - ck.T[:, None, :])
    l = jnp.where((k_pos[None, :] <= q_pos[:, None])[None], l, NEG)
    p = jax.nn.softmax(l, axis=-1)
    return jnp.einsum('hqk,khd->qhd', p, v.astype(F32)).astype(q.dtype)


def fox_prompt(q, k, v, logf):
    B, S, H, hd = q.shape
    c = jnp.cumsum(logf.astype(F32), axis=1)
    nq = S // QUERY_BLOCK
    qc = q.reshape(B, nq, QUERY_BLOCK, H, hd).swapaxes(0, 1)
    cc = c.reshape(B, nq, QUERY_BLOCK, H).swapaxes(0, 1)
    k_pos = jnp.arange(S, dtype=jnp.int32)

    def step(args):
        i, qb, cb = args
        q_pos = i * QUERY_BLOCK + jnp.arange(QUERY_BLOCK, dtype=jnp.int32)
        return jax.vmap(lambda a, b_, c_, d, e: fox_chunk(a, b_, c_, d, e, q_pos, k_pos))(qb, k, v, cb, c)

    out = lax.map(step, (jnp.arange(nq, dtype=jnp.int32), qc, cc))
    return out.swapaxes(0, 1).reshape(B, S, H, hd)


def fox_sample(q, k_all, v_all, logf_all, past_len):
    DS = q.shape[1]
    L = k_all.shape[1]
    c = jnp.cumsum(logf_all.astype(F32), axis=1)
    q_pos = past_len + jnp.arange(DS, dtype=jnp.int32)
    k_pos = jnp.arange(L, dtype=jnp.int32)
    return jax.vmap(lambda a, b_, c_, d, e: fox_chunk(a, b_, c_, d, e, q_pos, k_pos))(
        q, k_all, v_all, c[:, past_len:], c)


def dilated_chunk(q, k, v, q_pos, k_base):
    H, hd = q.shape[1], q.shape[2]
    slopes = alibi_slopes(H)
    scale = hd ** -0.5
    ms, ss, os_ = [], [], []
    for window, dil in DILATED_BRANCHES:
        n = window // dil + 1
        steps = dil * jnp.arange(n, dtype=jnp.int32)
        pos = q_pos[:, None] - steps[None, :]
        valid = pos >= 0
        idx = jnp.clip(pos - k_base, 0, k.shape[0] - 1)
        kg = k[idx]
        vg = v[idx]
        l = jnp.einsum('qhd,qnhd->hqn', q, kg, preferred_element_type=F32) * scale \
            - slopes[:, None, None] * steps.astype(F32)[None, None, :]
        l = jnp.where(valid[None], l, NEG)
        m = jnp.max(l, axis=-1)
        e = jnp.exp(l - m[..., None])
        s = jnp.sum(e, axis=-1)
        o = jnp.einsum('hqn,qnhd->hqd', e, vg.astype(F32)) / s[..., None]
        ms.append(m)
        ss.append(s)
        os_.append(o)
    m_all = jnp.stack(ms)
    w = jnp.stack(ss) * jnp.exp(m_all - jnp.max(m_all, axis=0, keepdims=True))
    out = jnp.sum(w[..., None] * jnp.stack(os_), axis=0) / jnp.sum(w, axis=0)[..., None]
    return out.transpose(1, 0, 2).astype(q.dtype)


def dilated_prompt(q, k, v):
    B, S, H, hd = q.shape
    nq = S // QUERY_BLOCK
    qc = q.reshape(B, nq, QUERY_BLOCK, H, hd).swapaxes(0, 1)

    def step(args):
        i, qb = args
        q_pos = i * QUERY_BLOCK + jnp.arange(QUERY_BLOCK, dtype=jnp.int32)
        return jax.vmap(lambda a, b_, c: dilated_chunk(a, b_, c, q_pos, 0))(qb, k, v)

    out = lax.map(step, (jnp.arange(nq, dtype=jnp.int32), qc))
    return out.swapaxes(0, 1).reshape(B, S, H, hd)


def dilated_sample(q, k_all, v_all, past_len, k_base):
    q_pos = past_len + jnp.arange(q.shape[1], dtype=jnp.int32)
    return lax.map(lambda a: dilated_chunk(a[0], a[1], a[2], q_pos, k_base), (q, k_all, v_all))


def peer_rows(x, w_q, subkeys, u, v):
    T = x.shape[0]
    q = jnp.einsum('td,de->te', x, w_q).reshape(T, PEER_HEADS, 2, PEER_DKEY // 2)
    s = jnp.einsum('thcd,hcnd->thcn', q, subkeys, preferred_element_type=F32)
    s1, i1 = lax.top_k(s[:, :, 0], PEER_TOPK)
    s2, i2 = lax.top_k(s[:, :, 1], PEER_TOPK)
    cand = (s1[..., :, None] + s2[..., None, :]).reshape(T, PEER_HEADS, PEER_TOPK * PEER_TOPK)
    cidx = (i1[..., :, None] * PEER_KEYS + i2[..., None, :]).reshape(T, PEER_HEADS, PEER_TOPK * PEER_TOPK)
    top_s, sel = lax.top_k(cand, PEER_TOPK)
    e = jnp.take_along_axis(cidx, sel, axis=-1)
    g = jax.nn.softmax(top_s, axis=-1)
    a = jax.nn.gelu(jnp.einsum('td,thkd->thk', x, u[e], preferred_element_type=F32), approximate=False)
    return jnp.einsum('thk,thkd->td', g * a, v[e].astype(F32)).astype(x.dtype)


def peer(x, w_q, subkeys, u, v):
    return lax.map(lambda xr: peer_rows(xr, w_q, subkeys, u, v), x)


def split_even(h, w_in, b_f):
    p = jnp.einsum('bsd,de->bse', h, w_in)
    cuts = [D_A, 2 * D_A, 3 * D_A, 3 * D_A + D_B, 3 * D_A + 2 * D_B, 3 * D_A + 3 * D_B]
    qa, ka, va, qb, kb, vb, fl = jnp.split(p, cuts, axis=-1)
    logf = jax.nn.log_sigmoid(fl.astype(F32) + b_f.astype(F32))
    return (heads(qa, N_HEADS_A), heads(ka, N_HEADS_A), heads(va, N_HEADS_A),
            heads(qb, N_HEADS_B), heads(kb, N_HEADS_B), heads(vb, N_HEADS_B), logf)


def split_odd(h, w_in):
    p = jnp.einsum('bsd,de->bse', h, w_in)
    q, k, v = jnp.split(p, [D_C, 2 * D_C], axis=-1)
    return heads(q, N_HEADS_C), heads(k, N_HEADS_C), heads(v, N_HEADS_C)


def gather_pages(pool, page_table):
    g = pool[page_table]
    return g.reshape((page_table.shape[0], -1) + pool.shape[2:])


def setup_inputs(seed: int = 0) -> dict:
    key = jax.random.key(seed)
    ks = jax.random.split(key, 24)
    n_pages = PAST_LEN // PAGE_SIZE
    n_pool = (5 * DEC_BATCH * n_pages) // 4
    win_buf = min(WIN_MAX, PAST_LEN)

    def nrm(k, shape, s=1.0):
        return jax.random.normal(k, shape, F32) * s

    page_table = jax.random.permutation(ks[9], n_pool)[:DEC_BATCH * n_pages]
    page_table = page_table.reshape(DEC_BATCH, n_pages).astype(jnp.int32)
    return {
        "x_prompt": nrm(ks[0], (BATCH, SEQ, D_MODEL)),
        "x_sample": nrm(ks[1], (DEC_BATCH, DEC_SEQ, D_MODEL)),
        "cache_moba_k": nrm(ks[2], (N_EVEN, n_pool, PAGE_SIZE, N_HEADS_A, HEAD_DIM)),
        "cache_moba_v": nrm(ks[3], (N_EVEN, n_pool, PAGE_SIZE, N_HEADS_A, HEAD_DIM)),
        "cache_fox_k": nrm(ks[4], (N_EVEN, n_pool, PAGE_SIZE, N_HEADS_B, HEAD_DIM)),
        "cache_fox_v": nrm(ks[5], (N_EVEN, n_pool, PAGE_SIZE, N_HEADS_B, HEAD_DIM)),
        "cache_fox_logf": jax.nn.log_sigmoid(FORGET_BIAS + nrm(ks[6], (N_EVEN, n_pool, PAGE_SIZE, N_HEADS_B))),
        "state_swa_k": nrm(ks[7], (N_ODD, DEC_BATCH, win_buf, N_HEADS_C, HEAD_DIM)),
        "state_swa_v": nrm(ks[8], (N_ODD, DEC_BATCH, win_buf, N_HEADS_C, HEAD_DIM)),
        "page_table": page_table,
        "norm_gain": 1.0 + 0.05 * nrm(ks[10], (DEPTH, 2, D_MODEL)),
        "final_gain": 1.0 + 0.05 * nrm(ks[11], (D_MODEL,)),
        "w_in_even": nrm(ks[12], (N_EVEN, D_MODEL, IN_EVEN), D_MODEL ** -0.5),
        "b_forget": FORGET_BIAS + 0.1 * nrm(ks[13], (N_EVEN, N_HEADS_B)),
        "w_out_even": nrm(ks[14], (N_EVEN, D_A + D_B, D_MODEL), (D_A + D_B) ** -0.5),
        "w_in_odd": nrm(ks[15], (N_ODD, D_MODEL, IN_ODD), D_MODEL ** -0.5),
        "w_out_odd": nrm(ks[16], (N_ODD, D_C, D_MODEL), D_C ** -0.5),
        "peer_w_query": nrm(ks[17], (DEPTH, D_MODEL, PEER_HEADS * PEER_DKEY), D_MODEL ** -0.5),
        "peer_subkeys": nrm(ks[18], (DEPTH, PEER_HEADS, 2, PEER_KEYS, PEER_DKEY // 2), (PEER_DKEY // 2) ** -0.5),
        "peer_u": nrm(ks[19], (DEPTH, PEER_EXPERTS, D_MODEL), D_MODEL ** -0.5),
        "peer_v": nrm(ks[20], (DEPTH, PEER_EXPERTS, D_MODEL), D_MODEL ** -0.5),
    }


def reference(x_prompt, x_sample, cache_moba_k, cache_moba_v, cache_fox_k, cache_fox_v, cache_fox_logf,
              state_swa_k, state_swa_v, page_table, norm_gain, final_gain, w_in_even, b_forget, w_out_even,
              w_in_odd, w_out_odd, peer_w_query, peer_subkeys, peer_u, peer_v):
    xp, xs = x_prompt, x_sample
    B, S, D = xp.shape
    DB, DS, _ = xs.shape
    past_len = page_table.shape[1] * cache_moba_k.shape[2]
    win_buf = state_swa_k.shape[2]
    keep_p = min(WIN_MAX, S)
    p_mk, p_mv, p_fk, p_fv, p_fl, p_sk, p_sv = [], [], [], [], [], [], []
    s_mk, s_mv, s_fk, s_fv, s_fl, s_sk, s_sv = [], [], [], [], [], [], []
    for layer in range(DEPTH):
        li = layer // 2
        hp = rmsnorm(xp, norm_gain[layer, 0])
        hs = rmsnorm(xs, norm_gain[layer, 0])
        if layer % 2 == 0:
            qa_p, ka_p, va_p, qb_p, kb_p, vb_p, lf_p = split_even(hp, w_in_even[li], b_forget[li])
            qa_s, ka_s, va_s, qb_s, kb_s, vb_s, lf_s = split_even(hs, w_in_even[li], b_forget[li])
            oa_p = moba_prompt(qa_p, ka_p, va_p)
            ob_p = fox_prompt(qb_p, kb_p, vb_p, lf_p)
            ka_all = jnp.concatenate([gather_pages(cache_moba_k[li], page_table), ka_s], axis=1)
            va_all = jnp.concatenate([gather_pages(cache_moba_v[li], page_table), va_s], axis=1)
            kb_all = jnp.concatenate([gather_pages(cache_fox_k[li], page_table), kb_s], axis=1)
            vb_all = jnp.concatenate([gather_pages(cache_fox_v[li], page_table), vb_s], axis=1)
            lf_all = jnp.concatenate([gather_pages(cache_fox_logf[li], page_table).astype(F32), lf_s], axis=1)
            oa_s = moba_sample(qa_s, ka_all, va_all, past_len)
            ob_s = fox_sample(qb_s, kb_all, vb_all, lf_all, past_len)
            mix_p = jnp.einsum('bse,ed->bsd', jnp.concatenate([oa_p, ob_p], axis=2).reshape(B, S, D_A + D_B), w_out_even[li])
            mix_s = jnp.einsum('bse,ed->bsd', jnp.concatenate([oa_s, ob_s], axis=2).reshape(DB, DS, D_A + D_B), w_out_even[li])
            p_mk.append(ka_p); p_mv.append(va_p); p_fk.append(kb_p); p_fv.append(vb_p); p_fl.append(lf_p)
            s_mk.append(ka_s); s_mv.append(va_s); s_fk.append(kb_s); s_fv.append(vb_s); s_fl.append(lf_s)
        else:
            qc_p, kc_p, vc_p = split_odd(hp, w_in_odd[li])
            qc_s, kc_s, vc_s = split_odd(hs, w_in_odd[li])
            oc_p = dilated_prompt(qc_p, kc_p, vc_p)
            kc_all = jnp.concatenate([state_swa_k[li], kc_s], axis=1)
            vc_all = jnp.concatenate([state_swa_v[li], vc_s], axis=1)
            oc_s = dilated_sample(qc_s, kc_all, vc_all, past_len, past_len - win_buf)
            mix_p = jnp.einsum('bse,ed->bsd', oc_p.reshape(B, S, D_C), w_out_odd[li])
            mix_s = jnp.einsum('bse,ed->bsd', oc_s.reshape(DB, DS, D_C), w_out_odd[li])
            p_sk.append(kc_p[:, S - keep_p:]); p_sv.append(vc_p[:, S - keep_p:])
            s_sk.append(kc_s); s_sv.append(vc_s)
        xp = xp + mix_p
        xs = xs + mix_s
        hp = rmsnorm(xp, norm_gain[layer, 1])
        hs = rmsnorm(xs, norm_gain[layer, 1])
        fp = peer(hp.reshape(B * S // PEER_ROW, PEER_ROW, D), peer_w_query[layer], peer_subkeys[layer],
                  peer_u[layer], peer_v[layer]).reshape(B, S, D)
        fs = peer(hs, peer_w_query[layer], peer_subkeys[layer], peer_u[layer], peer_v[layer])
        xp = xp + fp
        xs = xs + fs
    y_prompt = rmsnorm(xp, final_gain)
    y_sample = rmsnorm(xs, final_gain)
    return (y_prompt, y_sample,
            jnp.stack(p_mk), jnp.stack(p_mv), jnp.stack(p_fk), jnp.stack(p_fv), jnp.stack(p_fl),
            jnp.stack(p_sk), jnp.stack(p_sv),
            jnp.stack(s_mk), jnp.stack(s_mv), jnp.stack(s_fk), jnp.stack(s_fv), jnp.stack(s_fl),
            jnp.stack(s_sk), jnp.stack(s_sv))
```

```python
import functools
import math

import numpy as np
import jax
import jax.numpy as jnp
from jax import lax
from jax.experimental import pallas as pl
from jax.experimental.pallas import tpu as pltpu

F32 = jnp.float32
BF16 = jnp.bfloat16
HIGHEST = lax.Precision.HIGHEST

LANES = 128
SUBLANES = 8
VMEM_LIMIT_BYTES = 56 * 1024 * 1024

HEAD_DIM = 64
N_HEADS_A = 8
N_HEADS_B = 8
N_HEADS_C = 16
D_A = N_HEADS_A * HEAD_DIM
D_B = N_HEADS_B * HEAD_DIM
D_C = N_HEADS_C * HEAD_DIM
MOBA_BLOCK = 256
MOBA_TOPK = 3
DILATED_BRANCHES = ((128, 1), (512, 4), (2048, 16))
WIN_MAX = 2048
PEER_HEADS = 8
PEER_KEYS = 128
PEER_TOPK = 16
RMS_EPS = 1e-6
NEG = -1e30
SCALE = HEAD_DIM ** -0.5

ATT_TILE = 256
ROW_TILE = 512
PEER_TOKEN_TILE = 512
PEER_EXPERT_TILE = 512


def _cparams(*sem):
    return pltpu.CompilerParams(dimension_semantics=sem, vmem_limit_bytes=VMEM_LIMIT_BYTES)


def _alibi_slopes(n):
    return [2.0 ** (-8.0 * (i + 1) / n) for i in range(n)]


def _rms(x, g):
    return x * lax.rsqrt(jnp.mean(x * x, axis=-1, keepdims=True) + RMS_EPS) * g


def _rms_body(x_ref, g_ref, h_ref):
    h_ref[...] = _rms(x_ref[...], g_ref[...]).astype(h_ref.dtype)


def rmsnorm_rows(x, g, out_dtype):
    t, d = x.shape
    return pl.pallas_call(
        _rms_body,
        out_shape=jax.ShapeDtypeStruct((t, d), out_dtype),
        grid=(t // ROW_TILE,),
        in_specs=[pl.BlockSpec((ROW_TILE, d), lambda i: (i, 0)),
                  pl.BlockSpec((1, d), lambda i: (0, 0))],
        out_specs=pl.BlockSpec((ROW_TILE, d), lambda i: (i, 0)),
        compiler_params=_cparams("parallel"),
        name="rmsnorm_rows",
    )(x, g.reshape(1, d))


def _mm_body(h_ref, w_ref, o_ref):
    o_ref[...] = jnp.dot(h_ref[...], w_ref[...], preferred_element_type=F32)


def matmul(h, w, tn):
    t, k = h.shape
    n = w.shape[1]
    return pl.pallas_call(
        _mm_body,
        out_shape=jax.ShapeDtypeStruct((t, n), F32),
        grid=(t // ROW_TILE, n // tn),
        in_specs=[pl.BlockSpec((ROW_TILE, k), lambda i, j: (i, 0)),
                  pl.BlockSpec((k, tn), lambda i, j: (0, j))],
        out_specs=pl.BlockSpec((ROW_TILE, tn), lambda i, j: (i, j)),
        compiler_params=_cparams("parallel", "arbitrary"),
        name="matmul",
    )(h, w)


def _resid_norm_body(*refs, n_pairs):
    x_ref = refs[0]
    g_ref = refs[1 + 2 * n_pairs]
    xn_ref, h_ref, ht_ref = refs[2 + 2 * n_pairs:]
    acc = x_ref[...]
    for p in range(n_pairs):
        acc = acc + jnp.dot(refs[1 + 2 * p][...], refs[2 + 2 * p][...], preferred_element_type=F32)
    xn_ref[...] = acc
    h = _rms(acc, g_ref[...])
    h_ref[...] = h.astype(h_ref.dtype)
    ht_ref[...] = h.T.astype(ht_ref.dtype)


def resid_norm(x, pairs, g):
    t, d = x.shape
    args = [x]
    in_specs = [pl.BlockSpec((ROW_TILE, d), lambda i: (i, 0))]
    for a, w in pairs:
        args += [a, w]
        in_specs += [pl.BlockSpec((ROW_TILE, a.shape[1]), lambda i: (i, 0)),
                     pl.BlockSpec(w.shape, lambda i: (0, 0))]
    args.append(g.reshape(1, d))
    in_specs.append(pl.BlockSpec((1, d), lambda i: (0, 0)))
    return pl.pallas_call(
        functools.partial(_resid_norm_body, n_pairs=len(pairs)),
        out_shape=(jax.ShapeDtypeStruct((t, d), F32),
                   jax.ShapeDtypeStruct((t, d), BF16),
                   jax.ShapeDtypeStruct((d, t), BF16)),
        grid=(t // ROW_TILE,),
        in_specs=in_specs,
        out_specs=(pl.BlockSpec((ROW_TILE, d), lambda i: (i, 0)),
                   pl.BlockSpec((ROW_TILE, d), lambda i: (i, 0)),
                   pl.BlockSpec((d, ROW_TILE), lambda i: (0, i))),
        compiler_params=_cparams("parallel"),
        name="resid_norm",
    )(*args)


def _log_sigmoid(x):
    return -(jnp.maximum(-x, 0.0) + jnp.log1p(jnp.exp(-jnp.abs(x))))


def _fox_gate_body(p_ref, b_ref, lf_ref, c_ref, ct_ref, carry_sc, *, tiles_per_seq):
    i = pl.program_id(0)
    t = p_ref.shape[0]

    @pl.when(i % tiles_per_seq == 0)
    def _():
        carry_sc[...] = jnp.zeros_like(carry_sc)

    lf = _log_sigmoid(p_ref[...] + b_ref[...])
    lf_ref[...] = lf
    row = lax.broadcasted_iota(jnp.int32, (t, t), 0)
    col = lax.broadcasted_iota(jnp.int32, (t, t), 1)
    tri = (col <= row).astype(F32)
    c = jnp.dot(tri, lf, preferred_element_type=F32, precision=HIGHEST) + carry_sc[0:1, :]
    c_ref[...] = c
    ct_ref[...] = c.T[:SUBLANES, :]
    carry_sc[...] = jnp.broadcast_to(c[t - 1:t, :], carry_sc.shape)


def fox_gate(p, col_block, b_forget, seq_len):
    t = p.shape[0]
    tile = ATT_TILE
    b = jnp.zeros((1, LANES), F32).at[0, :b_forget.shape[0]].set(b_forget)
    return pl.pallas_call(
        functools.partial(_fox_gate_body, tiles_per_seq=seq_len // tile),
        out_shape=(jax.ShapeDtypeStruct((t, LANES), F32),
                   jax.ShapeDtypeStruct((t, LANES), F32),
                   jax.ShapeDtypeStruct((SUBLANES, t), F32)),
        grid=(t // tile,),
        in_specs=[pl.BlockSpec((tile, LANES), lambda i: (i, col_block)),
                  pl.BlockSpec((1, LANES), lambda i: (0, 0))],
        out_specs=(pl.BlockSpec((tile, LANES), lambda i: (i, 0)),
                   pl.BlockSpec((tile, LANES), lambda i: (i, 0)),
                   pl.BlockSpec((SUBLANES, tile), lambda i: (0, i))),
        scratch_shapes=[pltpu.VMEM((SUBLANES, LANES), F32)],
        compiler_params=_cparams("arbitrary"),
        name="fox_gate",
    )(p, b)


def _kmean_body(k_ref, o_ref):
    j = pl.program_id(1)
    o_ref[0, pl.ds(j, 1), :] = jnp.sum(k_ref[...], axis=0, keepdims=True) * (1.0 / MOBA_BLOCK)


def moba_block_means(p, k_col_block, batch, seq_len):
    nb = seq_len // MOBA_BLOCK
    return pl.pallas_call(
        _kmean_body,
        out_shape=jax.ShapeDtypeStruct((batch, nb, D_A), F32),
        grid=(batch, nb),
        in_specs=[pl.BlockSpec((MOBA_BLOCK, D_A), lambda b, j: (b * nb + j, k_col_block))],
        out_specs=pl.BlockSpec((1, nb, D_A), lambda b, j: (b, 0, 0)),
        compiler_params=_cparams("parallel", "arbitrary"),
        name="moba_block_means",
    )(p)


def _top_blocks(gate, n_valid, own, n_blocks):
    jidx = lax.broadcasted_iota(jnp.int32, gate.shape, 1).astype(F32)
    n_valid_f = n_valid.astype(F32) if hasattr(n_valid, "astype") else float(n_valid)
    own_f = own.astype(F32) if hasattr(own, "astype") else float(own)
    gate = jnp.where(jidx < n_valid_f, gate, NEG)
    sel = (jidx == own_f).astype(F32)
    for _ in range(MOBA_TOPK):
        mx = jnp.max(gate, axis=-1, keepdims=True)
        am = jnp.min(jnp.where(gate == mx, jidx, float(n_blocks)), axis=-1, keepdims=True)
        hit = jidx == am
        sel = jnp.where(jnp.logical_and(hit, am < n_valid_f), 1.0, sel)
        gate = jnp.where(hit, -jnp.inf, gate)
    return sel


def _flash_body(qi_tab, ki_tab, slopes, q_ref, k_ref, v_ref, *rest, mode, n_hp, win):
    if mode == "moba":
        km_ref, o_ref, qm_sc, acc_sc, m_sc, sel_sc = rest
    elif mode == "fox":
        cq_ref, ck_ref, o_ref, qm_sc, acc_sc, m_sc, cq_sc = rest
    else:
        mult_ref, o_ref, qm_sc, acc_sc, m_sc = rest
    t = q_ref.shape[0]
    hp = pl.program_id(0) % n_hp
    s = pl.program_id(1)
    qi = qi_tab[s]
    ki = ki_tab[s]
    delta = qi - ki
    lane = lax.broadcasted_iota(jnp.int32, (t, LANES), 1)
    lo = lane < HEAD_DIM

    @pl.when(ki == jnp.maximum(qi - win, 0))
    def _():
        q = q_ref[...]
        qs = q * SCALE
        qm_sc[0] = jnp.where(lo, qs, 0.0).astype(BF16)
        qm_sc[1] = jnp.where(lo, 0.0, qs).astype(BF16)
        acc_sc[...] = jnp.zeros_like(acc_sc)
        m_sc[...] = jnp.full_like(m_sc, NEG)
        if mode == "moba":
            km = km_ref[0]
            nb = km.shape[0]
            for h in range(2):
                qh = jnp.where(lo, q, 0.0) if h == 0 else jnp.where(lo, 0.0, q)
                gate = lax.dot_general(qh, km, (((1,), (1,)), ((), ())),
                                       preferred_element_type=F32, precision=HIGHEST)
                sel_sc[h] = _top_blocks(gate, qi, qi, nb).astype(BF16)
        if mode == "fox":
            cblk = cq_ref[...]
            for h in range(2):
                col = jnp.sum(jnp.where(lane == hp * 2 + h, cblk, 0.0), axis=-1, keepdims=True)
                cq_sc[h] = jnp.broadcast_to(col, (t, LANES))

    k2 = k_ref[...].astype(BF16)
    v2 = v_ref[...]
    row = lax.broadcasted_iota(jnp.int32, (t, t), 0)
    col = lax.broadcasted_iota(jnp.int32, (t, t), 1)
    rc = col - row
    if mode == "dil":
        mult = mult_ref[delta]
        ok = mult > 0.0
    else:
        ok = rc <= delta * t
    neg_dist = rc.astype(F32) - (delta * t).astype(F32)
    for h in range(2):
        sc = lax.dot_general(qm_sc[h], k2, (((1,), (1,)), ((), ())), preferred_element_type=F32)
        ok_h = ok
        if mode == "fox":
            ck = ck_ref[pl.ds(hp * 2 + h, 1), :]
            logit = sc + pltpu.repeat(cq_sc[h], t // LANES, axis=1) - ck
        else:
            logit = sc + slopes[hp * 2 + h] * neg_dist
        if mode == "moba":
            nb = sel_sc.shape[2]
            pick = (lax.broadcasted_iota(jnp.int32, (nb, t), 0) == ki).astype(BF16)
            vis = jnp.dot(sel_sc[h], pick, preferred_element_type=F32)
            ok_h = jnp.logical_and(ok, vis > 0.5)
        logit = jnp.where(ok_h, logit, NEG)
        m_old = m_sc[h]
        m_new = jnp.maximum(m_old, jnp.max(logit, axis=-1, keepdims=True))
        alpha = jnp.exp(m_old - m_new)
        p = jnp.exp(logit - pltpu.repeat(m_new, t // LANES, axis=1))
        if mode == "dil":
            p = p * mult
        own_half = lo if h == 0 else jnp.logical_not(lo)
        v_aug = jnp.where(own_half, v2, 1.0).astype(BF16)
        acc_sc[h] = alpha * acc_sc[h] + jnp.dot(p.astype(BF16), v_aug, preferred_element_type=F32)
        m_sc[h] = m_new

    @pl.when(ki == qi)
    def _():
        a0 = acc_sc[0]
        a1 = acc_sc[1]
        o_ref[...] = jnp.where(lo, a0 / a0[:, HEAD_DIM:HEAD_DIM + 1], a1 / a1[:, 0:1]).astype(o_ref.dtype)


def _dilated_multiplicity(n_tiles, t):
    d = (np.arange(n_tiles)[:, None, None] * t + np.arange(t)[None, :, None]
         - np.arange(t)[None, None, :])
    m = np.zeros(d.shape, np.float32)
    for window, dil in DILATED_BRANCHES:
        m += ((d >= 0) & (d <= window) & (d % dil == 0)).astype(np.float32)
    return m


def prompt_attention(p, mode, batch, seq_len, n_heads, q_blk, k_blk, v_blk, extra=()):
    t = ATT_TILE
    nq = seq_len // t
    n_hp = n_heads // 2
    win = (WIN_MAX // t) if mode == "dil" else nq
    steps = [(qi, ki) for qi in range(nq) for ki in range(max(0, qi - win), qi + 1)]
    qi_tab = jnp.asarray([s[0] for s in steps], jnp.int32)
    ki_tab = jnp.asarray([s[1] for s in steps], jnp.int32)
    slopes = jnp.asarray(_alibi_slopes(n_heads), F32)

    def qmap(g, s, qt, kt, sl):
        return ((g // n_hp) * nq + qt[s], q_blk + g % n_hp)

    def kmap(g, s, qt, kt, sl):
        return ((g // n_hp) * nq + kt[s], k_blk + g % n_hp)

    def vmap_(g, s, qt, kt, sl):
        return ((g // n_hp) * nq + kt[s], v_blk + g % n_hp)

    def omap(g, s, qt, kt, sl):
        return ((g // n_hp) * nq + qt[s], g % n_hp)

    in_specs = [pl.BlockSpec((t, LANES), qmap), pl.BlockSpec((t, LANES), kmap),
                pl.BlockSpec((t, LANES), vmap_)]
    scratch = [pltpu.VMEM((2, t, LANES), BF16), pltpu.VMEM((2, t, LANES), F32),
               pltpu.VMEM((2, t, LANES), F32)]
    args = [p, p, p]
    if mode == "moba":
        km, = extra
        nb = km.shape[1]
        in_specs.append(pl.BlockSpec((1, nb, LANES), lambda g, s, qt, kt, sl: (g // n_hp, 0, g % n_hp)))
        scratch.append(pltpu.VMEM((2, t, nb), BF16))
        args.append(km)
    elif mode == "fox":
        c, ct = extra
        in_specs.append(pl.BlockSpec((t, LANES), lambda g, s, qt, kt, sl: ((g // n_hp) * nq + qt[s], 0)))
        in_specs.append(pl.BlockSpec((SUBLANES, t), lambda g, s, qt, kt, sl: (0, (g // n_hp) * nq + kt[s])))
        scratch.append(pltpu.VMEM((2, t, LANES), F32))
        args += [c, ct]
    else:
        mult = jnp.asarray(_dilated_multiplicity(win + 1, t))
        in_specs.append(pl.BlockSpec(mult.shape, lambda g, s, qt, kt, sl: (0, 0, 0)))
        args.append(mult)
    return pl.pallas_call(
        functools.partial(_flash_body, mode=mode, n_hp=n_hp, win=win),
        out_shape=jax.ShapeDtypeStruct((batch * seq_len, n_heads * HEAD_DIM), BF16),
        grid_spec=pltpu.PrefetchScalarGridSpec(
            num_scalar_prefetch=3,
            grid=(batch * n_hp, len(steps)),
            in_specs=in_specs,
            out_specs=pl.BlockSpec((t, LANES), omap),
            scratch_shapes=scratch),
        compiler_params=_cparams("parallel", "arbitrary"),
        name="prompt_attention_" + mode,
    )(qi_tab, ki_tab, slopes, *args)


SAMPLE_PAGE_GROUP = 4


def _sample_block_means_body(pt_ref, *refs, group, pages_per_block):
    k_refs, o_ref = refs[:group], refs[group]
    s = pl.program_id(1)

    @pl.when(s == 0)
    def _():
        o_ref[...] = jnp.zeros_like(o_ref)

    lane = lax.broadcasted_iota(jnp.int32, o_ref.shape[1:], 2)
    for g in range(group):
        blk = (s * group + g) // pages_per_block
        part = jnp.sum(k_refs[g][0], axis=-1, keepdims=True) * (1.0 / MOBA_BLOCK)
        o_ref[0] = o_ref[0] + jnp.where(lane == blk, part, 0.0)


def sample_block_means(pool_t, page_table):
    db, n_pages = page_table.shape
    _, n_heads, _, page = pool_t.shape
    group = SAMPLE_PAGE_GROUP
    return pl.pallas_call(
        functools.partial(_sample_block_means_body, group=group, pages_per_block=MOBA_BLOCK // page),
        out_shape=jax.ShapeDtypeStruct((db, n_heads, HEAD_DIM, LANES), F32),
        grid_spec=pltpu.PrefetchScalarGridSpec(
            num_scalar_prefetch=1,
            grid=(db, n_pages // group),
            in_specs=[pl.BlockSpec((1, n_heads, HEAD_DIM, page),
                                   functools.partial(lambda b, s, pt, g: (pt[b * n_pages + s * group + g], 0, 0, 0), g=g))
                      for g in range(group)],
            out_specs=pl.BlockSpec((1, n_heads, HEAD_DIM, LANES), lambda b, s, pt: (b, 0, 0, 0))),
        compiler_params=_cparams("parallel", "arbitrary"),
        name="sample_block_means",
    )(page_table.reshape(-1), *([pool_t] * group))


def _sample_body(pt_ref, q_ref, kn_ref, vn_ref, *rest, mode, n_heads, n_steps, group, page, pages_per_block):
    kp, vp, rest = rest[:group], rest[group:2 * group], rest[2 * group:]
    if mode == "moba":
        bias_ref, biasn_ref, kms_ref, o_ref, qbd_sc, kn_sc, vn_sc, acc_sc, m_sc, l_sc, sel_sc = rest
    elif mode == "fox":
        lfp, rest = rest[:group], rest[group:]
        lfn_ref, o_ref, qbd_sc, kn_sc, vn_sc, acc_sc, m_sc, l_sc, ncb_sc, run_sc = rest
    else:
        bias_ref, biasn_ref, mult_ref, multn_ref, o_ref, qbd_sc, kn_sc, vn_sc, acc_sc, m_sc, l_sc = rest
    s = pl.program_id(1)
    nq = q_ref.shape[0]
    rows, width = acc_sc.shape
    lane = lax.broadcasted_iota(jnp.int32, (rows, LANES), 1)
    row_q = lax.broadcasted_iota(jnp.int32, (rows, LANES), 0) % nq
    col_head = lax.broadcasted_iota(jnp.int32, (nq, width), 1) // HEAD_DIM

    def attend(scores, pv, bias, ok, mult):
        logit = scores + bias
        if ok is not None:
            logit = jnp.where(ok, logit, NEG)
        m_old = m_sc[...]
        m_new = jnp.maximum(m_old, jnp.max(logit, axis=-1, keepdims=True))
        alpha = jnp.exp(m_old - m_new)
        p = jnp.exp(logit - m_new)
        if mult is not None:
            p = p * mult
        l_sc[...] = alpha * l_sc[...] + jnp.sum(p, axis=-1, keepdims=True)
        acc_sc[...] = pltpu.repeat(alpha, width // LANES, axis=1) * acc_sc[...] + pv(p.astype(BF16))
        m_sc[...] = m_new

    @pl.when(s == 0)
    def _():
        acc_sc[...] = jnp.zeros_like(acc_sc)
        l_sc[...] = jnp.zeros_like(l_sc)
        m_sc[...] = jnp.full_like(m_sc, NEG)
        kn_sc[...] = jnp.zeros_like(kn_sc)
        vn_sc[...] = jnp.zeros_like(vn_sc)
        kn_sc[0:nq, :] = kn_ref[...].astype(BF16)
        vn_sc[0:nq, :] = vn_ref[...].astype(BF16)
        q = q_ref[...]
        for h in range(n_heads):
            qbd_sc[h * nq:(h + 1) * nq, :] = jnp.where(col_head == h, q * SCALE, 0.0).astype(BF16)
        causal = lane <= row_q
        scores = lax.dot_general(qbd_sc[...], kn_sc[...], (((1,), (1,)), ((), ())), preferred_element_type=F32)
        pv_new = lambda p: jnp.dot(p, vn_sc[...], preferred_element_type=F32)
        if mode == "moba":
            gate = jnp.concatenate(
                [jnp.dot(q[:, h * HEAD_DIM:(h + 1) * HEAD_DIM], kms_ref[0, h],
                         preferred_element_type=F32, precision=HIGHEST) for h in range(n_heads)], axis=0)
            nb = n_steps * group // pages_per_block
            sel_sc[...] = _top_blocks(gate, nb, nb, LANES)
            attend(scores, pv_new, biasn_ref[...], causal, None)
        elif mode == "fox":
            lfn = lfn_ref[...]
            tri = (lax.broadcasted_iota(jnp.int32, (nq, nq), 1)
                   <= lax.broadcasted_iota(jnp.int32, (nq, nq), 0)).astype(F32)
            newcum = jnp.dot(tri, lfn, preferred_element_type=F32, precision=HIGHEST)
            eye = lane[:nq] == row_q[:nq]
            ncb, nrow = [], []
            for h in range(n_heads):
                col = jnp.broadcast_to(newcum[:, h:h + 1], (nq, LANES))
                ncb.append(col)
                nrow.append(jnp.broadcast_to(jnp.sum(jnp.where(eye, col, 0.0), axis=0, keepdims=True), (nq, LANES)))
            ncb = jnp.concatenate(ncb, axis=0)
            ncb_sc[...] = ncb
            run_sc[...] = jnp.zeros_like(run_sc)
            attend(scores, pv_new, ncb - jnp.concatenate(nrow, axis=0), causal, None)
        else:
            multn = multn_ref[...]
            attend(scores, pv_new, biasn_ref[...], multn > 0.0, multn)

    for g in range(group):
        k_t = kp[g][0].reshape(width, page).astype(BF16)
        v_t = vp[g][0].reshape(width, page).astype(BF16)
        scores = jnp.dot(qbd_sc[...], k_t, preferred_element_type=F32)
        pv = lambda p: lax.dot_general(p, v_t, (((1,), (1,)), ((), ())), preferred_element_type=F32)
        if mode == "moba":
            blk = (s * group + g) // pages_per_block
            vis = jnp.max(jnp.where(lane == blk, sel_sc[...], 0.0), axis=-1, keepdims=True)
            attend(scores, pv, bias_ref[g], vis > 0.5, None)
        elif mode == "fox":
            lft = lfp[g][0]
            upper = (lax.broadcasted_iota(jnp.int32, (page, page), 0)
                     > lax.broadcasted_iota(jnp.int32, (page, page), 1)).astype(F32)
            after = jnp.dot(lft, upper, preferred_element_type=F32, precision=HIGHEST) + run_sc[...]
            after_rows = jnp.concatenate(
                [jnp.broadcast_to(after[h:h + 1, :], (nq, page)) for h in range(n_heads)], axis=0)
            attend(scores, pv, ncb_sc[...] + after_rows, None, None)
            run_sc[...] = run_sc[...] + jnp.sum(lft, axis=-1, keepdims=True)
        else:
            mult = mult_ref[g]
            attend(scores, pv, bias_ref[g], mult > 0.0, mult)

    @pl.when(s == n_steps - 1)
    def _():
        out = acc_sc[...] / pltpu.repeat(l_sc[...], width // LANES, axis=1)
        o_ref[...] = sum(jnp.where(col_head == h, out[h * nq:(h + 1) * nq, :], 0.0) for h in range(n_heads))


def _multiplicity(d):
    m = np.zeros(d.shape, np.float32)
    for window, dil in DILATED_BRANCHES:
        m += ((d >= 0) & (d <= window) & (d % dil == 0)).astype(np.float32)
    return m


def _sample_tables(past_len, nq, n_heads, n_pages, page):
    d = (past_len + np.arange(nq)[None, :, None]) - (np.arange(n_pages)[:, None, None] * page
                                                     + np.arange(page)[None, None, :])
    dn = np.arange(nq)[:, None] - np.arange(page)[None, :]
    slopes = np.repeat(np.asarray(_alibi_slopes(n_heads), np.float64), nq)[None, :, None]
    tile = lambda a: np.tile(a, (1, n_heads, 1))
    d, dn = tile(d), tile(dn[None])
    return ((-slopes * d).astype(np.float32), (-slopes * dn)[0].astype(np.float32),
            _multiplicity(d), _multiplicity(dn)[0])


def sample_attention(p, mode, row0, db, nq, n_heads, q_blk, k_blk, v_blk, k_pool_t, v_pool_t, page_table, extra=()):
    width = n_heads * HEAD_DIM
    rows = n_heads * nq
    group = SAMPLE_PAGE_GROUP
    reverse = mode == "fox"
    paged = page_table is not None
    page = k_pool_t.shape[3] if paged else LANES
    n_pages = page_table.shape[1] if paged else k_pool_t.shape[3] // page
    pt = page_table.reshape(-1) if paged else jnp.zeros((1,), jnp.int32)
    n_steps = n_pages // group
    rb0 = row0 // nq

    def pg(s, g):
        j = s * group + g
        return (n_pages - 1 - j) if reverse else j

    def pool_spec(g):
        if paged:
            return pl.BlockSpec((1, n_heads, HEAD_DIM, page), lambda b, s, t: (t[b * n_pages + pg(s, g)], 0, 0, 0))
        return pl.BlockSpec((1, n_heads, HEAD_DIM, page), lambda b, s, t: (b, 0, 0, pg(s, g)))

    bias, biasn, mult, multn = [jnp.asarray(a) for a in _sample_tables(n_pages * page, nq, n_heads, n_pages, page)]
    table_spec = pl.BlockSpec((group, rows, page), lambda b, s, t: (s, 0, 0))
    new_spec = pl.BlockSpec((rows, page), lambda b, s, t: (0, 0))
    in_specs = [pl.BlockSpec((nq, width), lambda b, s, t: (rb0 + b, q_blk)),
                pl.BlockSpec((nq, width), lambda b, s, t: (rb0 + b, k_blk)),
                pl.BlockSpec((nq, width), lambda b, s, t: (rb0 + b, v_blk))]
    in_specs += [pool_spec(g) for g in range(group)] * 2
    args = [p, p, p] + [k_pool_t] * group + [v_pool_t] * group
    scratch = [pltpu.VMEM((rows, width), BF16), pltpu.VMEM((page, width), BF16), pltpu.VMEM((page, width), BF16),
               pltpu.VMEM((rows, width), F32), pltpu.VMEM((rows, LANES), F32), pltpu.VMEM((rows, LANES), F32)]
    if mode == "moba":
        kms, = extra
        in_specs += [table_spec, new_spec, pl.BlockSpec((1,) + kms.shape[1:], lambda b, s, t: (b, 0, 0, 0))]
        args += [bias, biasn, kms]
        scratch.append(pltpu.VMEM((rows, LANES), F32))
    elif mode == "fox":
        lf_pool_t, logf = extra
        in_specs += [pl.BlockSpec((1, n_heads, page),
                                  functools.partial(lambda b, s, t, g: (t[b * n_pages + pg(s, g)], 0, 0), g=g))
                     for g in range(group)]
        in_specs.append(pl.BlockSpec((nq, LANES), lambda b, s, t: (rb0 + b, 0)))
        args += [lf_pool_t] * group + [logf]
        scratch += [pltpu.VMEM((rows, LANES), F32), pltpu.VMEM((n_heads, LANES), F32)]
    else:
        in_specs += [table_spec, new_spec, table_spec, new_spec]
        args += [bias, biasn, mult, multn]
    return pl.pallas_call(
        functools.partial(_sample_body, mode=mode, n_heads=n_heads, n_steps=n_steps, group=group, page=page,
                          pages_per_block=MOBA_BLOCK // page),
        out_shape=jax.ShapeDtypeStruct((db * nq, width), F32),
        grid_spec=pltpu.PrefetchScalarGridSpec(
            num_scalar_prefetch=1,
            grid=(db, n_steps),
            in_specs=in_specs,
            out_specs=pl.BlockSpec((nq, width), lambda b, s, t: (b, 0)),
            scratch_shapes=scratch),
        compiler_params=_cparams("parallel", "arbitrary"),
        name="sample_attention_" + mode,
    )(pt, *args)


def _top_rows(x, k):
    vals = []
    for _ in range(k):
        m = jnp.max(x, axis=0, keepdims=True)
        vals.append(m)
        x = jnp.where(x == m, -jnp.inf, x)
    return vals


def _peer_scores_body(ht_ref, wq_ref, sk_ref, s1_ref, s2_ref, e1_ref, e2_ref, tau_ref):
    half = PEER_KEYS
    qt = jnp.dot(wq_ref[...], ht_ref[...], preferred_element_type=F32)
    s1 = jnp.dot(sk_ref[0, 0], qt[:half], preferred_element_type=F32, precision=HIGHEST)
    s2 = jnp.dot(sk_ref[0, 1], qt[half:], preferred_element_type=F32, precision=HIGHEST)
    t1 = _top_rows(s1, PEER_TOPK)
    t2 = _top_rows(s2, PEER_TOPK)
    cand = [t1[i] + t2[j] for i in range(PEER_TOPK) for j in range(PEER_TOPK // (i + 1))]
    pad = -len(cand) % SUBLANES
    cand = jnp.concatenate(cand + [jnp.full_like(t1[0], -jnp.inf)] * pad, axis=0)
    top = _top_rows(cand, PEER_TOPK)
    z = sum(jnp.exp(v - top[0]) for v in top)
    s1_ref[0] = s1
    s2_ref[0] = s2
    e1_ref[0] = jnp.exp(s1 - t1[0]) / z
    e2_ref[0] = jnp.exp(s2 - t2[0])
    tau_ref[0] = jnp.broadcast_to(top[PEER_TOPK - 1], tau_ref.shape[1:])


def peer_scores(ht, wq_t, subkeys):
    d, t = ht.shape
    tt = PEER_TOKEN_TILE
    nh = PEER_HEADS
    dq = wq_t.shape[0] // nh
    big = jax.ShapeDtypeStruct((nh, PEER_KEYS, t), F32)
    bspec = pl.BlockSpec((1, PEER_KEYS, tt), lambda i, h: (h, 0, i))
    return pl.pallas_call(
        _peer_scores_body,
        out_shape=(big, big, big, big, jax.ShapeDtypeStruct((nh, SUBLANES, t), F32)),
        grid=(t // tt, nh),
        in_specs=[pl.BlockSpec((d, tt), lambda i, h: (0, i)),
                  pl.BlockSpec((dq, d), lambda i, h: (h, 0)),
                  pl.BlockSpec((1,) + subkeys.shape[1:], lambda i, h: (h, 0, 0, 0))],
        out_specs=(bspec, bspec, bspec, bspec, pl.BlockSpec((1, SUBLANES, tt), lambda i, h: (h, 0, i))),
        compiler_params=_cparams("parallel", "arbitrary"),
        name="peer_scores",
    )(ht, wq_t, subkeys)


def _gelu(x):
    return 0.5 * x * (1.0 + lax.erf(x * (1.0 / math.sqrt(2.0))))


def _peer_experts_body(ht_ref, u_ref, vt_ref, s1_ref, s2_ref, e1_ref, e2_ref, tau_ref, x_ref, g_ref,
                       xn_ref, hn_ref, acc_sc, w_sc):
    j = pl.program_id(1)
    te = u_ref.shape[0]
    tt = ht_ref.shape[1]
    groups = te // PEER_KEYS

    @pl.when(j == 0)
    def _():
        acc_sc[...] = jnp.zeros_like(acc_sc)

    act = _gelu(jnp.dot(u_ref[...], ht_ref[...], preferred_element_type=F32))
    for al in range(groups):
        a = j * groups + al
        coef = jnp.zeros((PEER_KEYS, tt), F32)
        for h in range(PEER_HEADS):
            score = s2_ref[h] + s1_ref[h, pl.ds(a, 1), :]
            coef = coef + jnp.where(score >= tau_ref[h, 0:1, :], e2_ref[h], 0.0) * e1_ref[h, pl.ds(a, 1), :]
        rows = slice(al * PEER_KEYS, (al + 1) * PEER_KEYS)
        w_sc[rows, :] = (coef * act[rows, :]).astype(BF16)
    acc_sc[...] += jnp.dot(vt_ref[...], w_sc[...], preferred_element_type=F32)

    @pl.when(j == pl.num_programs(1) - 1)
    def _():
        xn = x_ref[...] + acc_sc[...].T
        xn_ref[...] = xn
        hn_ref[...] = _rms(xn, g_ref[...]).astype(hn_ref.dtype)


def peer_experts(x, ht, u, v_t, scores, g_next, norm_dtype):
    t, d = x.shape
    tt = PEER_TOKEN_TILE
    te = PEER_EXPERT_TILE
    n_exp = u.shape[0]
    s1, s2, e1, e2, tau = scores
    sspec = pl.BlockSpec((PEER_HEADS, PEER_KEYS, tt), lambda i, j: (0, 0, i))
    return pl.pallas_call(
        _peer_experts_body,
        out_shape=(jax.ShapeDtypeStruct((t, d), F32), jax.ShapeDtypeStruct((t, d), norm_dtype)),
        grid=(t // tt, n_exp // te),
        in_specs=[pl.BlockSpec((d, tt), lambda i, j: (0, i)),
                  pl.BlockSpec((te, d), lambda i, j: (j, 0)),
                  pl.BlockSpec((d, te), lambda i, j: (0, j)),
                  sspec, sspec, sspec, sspec,
                  pl.BlockSpec((PEER_HEADS, SUBLANES, tt), lambda i, j: (0, 0, i)),
                  pl.BlockSpec((tt, d), lambda i, j: (i, 0)),
                  pl.BlockSpec((1, d), lambda i, j: (0, 0))],
        out_specs=(pl.BlockSpec((tt, d), lambda i, j: (i, 0)),
                   pl.BlockSpec((tt, d), lambda i, j: (i, 0))),
        scratch_shapes=[pltpu.VMEM((d, tt), F32), pltpu.VMEM((te, tt), BF16)],
        compiler_params=_cparams("parallel", "arbitrary"),
        name="peer_experts",
    )(ht, u, v_t, s1, s2, e1, e2, tau, x, g_next.reshape(1, d))


def _heads(a, n):
    return a.reshape(a.shape[:-1] + (n, HEAD_DIM))


def kernel(x_prompt, x_sample, cache_moba_k, cache_moba_v, cache_fox_k, cache_fox_v, cache_fox_logf,
           state_swa_k, state_swa_v, page_table, norm_gain, final_gain, w_in_even, b_forget, w_out_even,
           w_in_odd, w_out_odd, peer_w_query, peer_subkeys, peer_u, peer_v):
    b, s, d = x_prompt.shape
    db, ds, _ = x_sample.shape
    tp = b * s
    depth = norm_gain.shape[0]
    n_pool, page = cache_moba_k.shape[1:3]
    past_len = page_table.shape[1] * page
    win_buf = state_swa_k.shape[2]
    keep_p = min(WIN_MAX, s)
    assert win_buf == past_len and past_len % MOBA_BLOCK == 0 and MOBA_BLOCK % page == 0
    assert s % ATT_TILE == 0 and (tp + db * ds) % ROW_TILE == 0 and tp % ds == 0

    x = jnp.concatenate([x_prompt.reshape(tp, d), x_sample.reshape(db * ds, d)], axis=0)
    h = rmsnorm_rows(x, norm_gain[0, 0], BF16)
    new = {}
    for layer in range(depth):
        li = layer // 2
        if layer % 2 == 0:
            n_in = w_in_even.shape[2]
            n_pad = -n_in % (5 * LANES)
            w_in = jnp.pad(w_in_even[li], ((0, 0), (0, n_pad))).astype(BF16)
            p = matmul(h, w_in, (n_in + n_pad) // 5)
            logf, csum, csum_t = fox_gate(p, 3 * (D_A + D_B) // LANES, b_forget[li], s)
            km = moba_block_means(p, 1, b, s)
            oa_p = prompt_attention(p, "moba", b, s, N_HEADS_A, 0, D_A // LANES, 2 * D_A // LANES, (km,))
            fb = 3 * D_A // LANES
            ob_p = prompt_attention(p, "fox", b, s, N_HEADS_B, fb, fb + D_B // LANES, fb + 2 * D_B // LANES,
                                    (csum, csum_t))
            pool = lambda c: jnp.transpose(c[li], (0, 2, 3, 1))
            ka_pool, va_pool = pool(cache_moba_k), pool(cache_moba_v)
            kb_pool, vb_pool = pool(cache_fox_k), pool(cache_fox_v)
            kms = sample_block_means(ka_pool, page_table)
            oa_s = sample_attention(p, "moba", tp, db, ds, N_HEADS_A, 0, 1, 2, ka_pool, va_pool, page_table, (kms,))
            lf_pool_t = jnp.swapaxes(cache_fox_logf[li], 1, 2)
            ob_s = sample_attention(p, "fox", tp, db, ds, N_HEADS_B, 3, 4, 5, kb_pool, vb_pool, page_table,
                                    (lf_pool_t, logf))
            w_out = w_out_even[li].astype(BF16)
            pairs = [(jnp.concatenate([oa_p, oa_s.astype(BF16)], axis=0), w_out[:D_A]),
                     (jnp.concatenate([ob_p, ob_s.astype(BF16)], axis=0), w_out[D_A:])]
            cuts = {"mk": (D_A, N_HEADS_A), "mv": (2 * D_A, N_HEADS_A), "fk": (3 * D_A + D_B, N_HEADS_B),
                    "fv": (3 * D_A + 2 * D_B, N_HEADS_B)}
            for name, (c0, nh) in cuts.items():
                cols = p[:, c0:c0 + nh * HEAD_DIM]
                new.setdefault("p_" + name, []).append(_heads(cols[:tp].reshape(b, s, -1), nh))
                new.setdefault("s_" + name, []).append(_heads(cols[tp:].reshape(db, ds, -1), nh))
            new.setdefault("p_fl", []).append(logf[:tp, :N_HEADS_B].reshape(b, s, N_HEADS_B))
            new.setdefault("s_fl", []).append(logf[tp:, :N_HEADS_B].reshape(db, ds, N_HEADS_B))
        else:
            p = matmul(h, w_in_odd[li].astype(BF16), D_C // 2)
            nb = D_C // LANES
            oc_p = prompt_attention(p, "dil", b, s, N_HEADS_C, 0, nb, 2 * nb)
            kc_pool = jnp.transpose(state_swa_k[li], (0, 2, 3, 1))
            vc_pool = jnp.transpose(state_swa_v[li], (0, 2, 3, 1))
            oc_s = sample_attention(p, "dil", tp, db, ds, N_HEADS_C, 0, 1, 2, kc_pool, vc_pool, None)
            pairs = [(jnp.concatenate([oc_p, oc_s.astype(BF16)], axis=0), w_out_odd[li].astype(BF16))]
            for name, c0 in (("sk", D_C), ("sv", 2 * D_C)):
                cols = p[:, c0:c0 + D_C]
                new.setdefault("p_" + name, []).append(
                    _heads(cols[:tp].reshape(b, s, -1)[:, s - keep_p:], N_HEADS_C))
                new.setdefault("s_" + name, []).append(_heads(cols[tp:].reshape(db, ds, -1), N_HEADS_C))
        x, _, h_t = resid_norm(x, pairs, norm_gain[layer, 1])
        scores = peer_scores(h_t, peer_w_query[layer].T.astype(BF16), peer_subkeys[layer])
        last = layer == depth - 1
        g_next = final_gain if last else norm_gain[layer + 1, 0]
        x, h = peer_experts(x, h_t, peer_u[layer].astype(BF16), peer_v[layer].T.astype(BF16), scores,
                            g_next, F32 if last else BF16)
    y = h
    order = ["p_mk", "p_mv", "p_fk", "p_fv", "p_fl", "p_sk", "p_sv",
             "s_mk", "s_mv", "s_fk", "s_fv", "s_fl", "s_sk", "s_sv"]
    return (y[:tp].reshape(b, s, d), y[tp:].reshape(db, ds, d)) + tuple(jnp.stack(new[k]) for k in order)
```

```python
import functools
import math

import numpy as np
import jax
import jax.numpy as jnp
from jax import lax
from jax.experimental import pallas as pl
from jax.experimental.pallas import tpu as pltpu

F32 = jnp.float32
BF16 = jnp.bfloat16
HIGHEST = lax.Precision.HIGHEST

LANES = 128
SUBLANES = 8
VMEM_LIMIT_BYTES = 56 * 1024 * 1024

HEAD_DIM = 64
N_HEADS_A = 8
N_HEADS_B = 8
N_HEADS_C = 16
D_A = N_HEADS_A * HEAD_DIM
D_B = N_HEADS_B * HEAD_DIM
D_C = N_HEADS_C * HEAD_DIM
MOBA_BLOCK = 256
MOBA_TOPK = 3
DILATED_BRANCHES = ((128, 1), (512, 4), (2048, 16))
WIN_MAX = 2048
PEER_HEADS = 8
PEER_KEYS = 128
PEER_TOPK = 16
RMS_EPS = 1e-6
NEG = -1e30
SCALE = HEAD_DIM ** -0.5
LOG2E = 1.0 / math.log(2.0)

ATT_TILE = 512
GATE_TILE = 256
ROW_TILE = 512
PEER_TOKEN_TILE = 512
PEER_EXPERT_TILE = 512
SAMPLE_PAGE_GROUP = 8

X_POS_HI = 0
X_POS_LO = 3
X_FORGET = 0
X_BLOCK = 8
POS_SPLIT = 16
N_PIECES = 3


def _cparams(*sem):
    return pltpu.CompilerParams(dimension_semantics=sem, vmem_limit_bytes=VMEM_LIMIT_BYTES)


def _alibi_slopes(n):
    return [2.0 ** (-8.0 * (i + 1) / n) for i in range(n)]


def _bf16_pieces_np(x):
    x = np.asarray(x, np.float32)
    out = []
    for _ in range(N_PIECES):
        p = np.asarray(np.asarray(x, dtype=BF16), np.float32)
        out.append(p)
        x = x - p
    return out


def _rms(x, g):
    return x * lax.rsqrt(jnp.mean(x * x, axis=-1, keepdims=True) + RMS_EPS) * g


def _rms_body(x_ref, g_ref, h_ref):
    h_ref[...] = _rms(x_ref[...], g_ref[...]).astype(h_ref.dtype)


def rmsnorm_rows(x, g, out_dtype):
    t, d = x.shape
    return pl.pallas_call(
        _rms_body,
        out_shape=jax.ShapeDtypeStruct((t, d), out_dtype),
        grid=(t // ROW_TILE,),
        in_specs=[pl.BlockSpec((ROW_TILE, d), lambda i: (i, 0)),
                  pl.BlockSpec((1, d), lambda i: (0, 0))],
        out_specs=pl.BlockSpec((ROW_TILE, d), lambda i: (i, 0)),
        compiler_params=_cparams("parallel"),
        name="rmsnorm_rows",
    )(x, g.reshape(1, d))


def _mm_body(h_ref, w_ref, o_ref):
    o_ref[...] = jnp.dot(h_ref[...], w_ref[...], preferred_element_type=F32)


def matmul(h, w, tn):
    t, k = h.shape
    n = w.shape[1]
    return pl.pallas_call(
        _mm_body,
        out_shape=jax.ShapeDtypeStruct((t, n), F32),
        grid=(t // ROW_TILE, n // tn),
        in_specs=[pl.BlockSpec((ROW_TILE, k), lambda i, j: (i, 0)),
                  pl.BlockSpec((k, tn), lambda i, j: (0, j))],
        out_specs=pl.BlockSpec((ROW_TILE, tn), lambda i, j: (i, j)),
        compiler_params=_cparams("parallel", "arbitrary"),
        name="matmul",
    )(h, w)


def _resid_norm_body(*refs, n_pairs):
    x_ref = refs[0]
    g_ref = refs[1 + 2 * n_pairs]
    xn_ref, h_ref, ht_ref = refs[2 + 2 * n_pairs:]
    acc = x_ref[...]
    for p in range(n_pairs):
        acc = acc + jnp.dot(refs[1 + 2 * p][...], refs[2 + 2 * p][...], preferred_element_type=F32)
    xn_ref[...] = acc
    h = _rms(acc, g_ref[...])
    h_ref[...] = h.astype(h_ref.dtype)
    ht_ref[...] = h.T.astype(ht_ref.dtype)


def resid_norm(x, pairs, g):
    t, d = x.shape
    args = [x]
    in_specs = [pl.BlockSpec((ROW_TILE, d), lambda i: (i, 0))]
    for a, w in pairs:
        args += [a, w]
        in_specs += [pl.BlockSpec((ROW_TILE, a.shape[1]), lambda i: (i, 0)),
                     pl.BlockSpec(w.shape, lambda i: (0, 0))]
    args.append(g.reshape(1, d))
    in_specs.append(pl.BlockSpec((1, d), lambda i: (0, 0)))
    return pl.pallas_call(
        functools.partial(_resid_norm_body, n_pairs=len(pairs)),
        out_shape=(jax.ShapeDtypeStruct((t, d), F32),
                   jax.ShapeDtypeStruct((t, d), BF16),
                   jax.ShapeDtypeStruct((d, t), BF16)),
        grid=(t // ROW_TILE,),
        in_specs=in_specs,
        out_specs=(pl.BlockSpec((ROW_TILE, d), lambda i: (i, 0)),
                   pl.BlockSpec((ROW_TILE, d), lambda i: (i, 0)),
                   pl.BlockSpec((d, ROW_TILE), lambda i: (0, i))),
        compiler_params=_cparams("parallel"),
        name="resid_norm",
    )(*args)


def _log_sigmoid(x):
    return -(jnp.maximum(-x, 0.0) + jnp.log1p(jnp.exp(-jnp.abs(x))))


def _fox_gate_body(p_ref, b_ref, place_ref, lf_ref, kx_ref, carry_sc, *, tiles_per_seq):
    i = pl.program_id(0)
    t = p_ref.shape[0]

    @pl.when(i % tiles_per_seq == 0)
    def _():
        carry_sc[...] = jnp.zeros_like(carry_sc)

    lf = _log_sigmoid(p_ref[...] + b_ref[...])
    lf_ref[...] = lf
    row = lax.broadcasted_iota(jnp.int32, (t, t), 0)
    col = lax.broadcasted_iota(jnp.int32, (t, t), 1)
    tri = (col <= row).astype(F32)
    c = jnp.dot(tri, lf, preferred_element_type=F32, precision=HIGHEST) + carry_sc[0:1, :]
    carry_sc[...] = jnp.broadcast_to(c[t - 1:t, :], carry_sc.shape)
    rest = c * LOG2E
    pieces = []
    for _ in range(N_PIECES):
        piece = rest.astype(BF16)
        pieces.append(piece)
        rest = rest - piece.astype(F32)
    for hp in range(kx_ref.shape[0]):
        moved = sum(jnp.dot(pieces[k], place_ref[hp, k], preferred_element_type=F32) for k in range(N_PIECES))
        kx_ref[hp] = (-moved).astype(BF16)


def fox_gate(p, col_block, b_forget, seq_len):
    t = p.shape[0]
    n_heads = b_forget.shape[0]
    n_hp = n_heads // 2
    b = jnp.zeros((1, LANES), F32).at[0, :n_heads].set(b_forget)
    place = np.zeros((n_hp, N_PIECES, LANES, LANES), np.float32)
    for hp in range(n_hp):
        for k in range(N_PIECES):
            place[hp, k, 2 * hp, HEAD_DIM + X_FORGET + k] = 1.0
            place[hp, k, 2 * hp + 1, X_FORGET + k] = 1.0
    return pl.pallas_call(
        functools.partial(_fox_gate_body, tiles_per_seq=seq_len // GATE_TILE),
        out_shape=(jax.ShapeDtypeStruct((t, LANES), F32),
                   jax.ShapeDtypeStruct((n_hp, t, LANES), BF16)),
        grid=(t // GATE_TILE,),
        in_specs=[pl.BlockSpec((GATE_TILE, LANES), lambda i: (i, col_block)),
                  pl.BlockSpec((1, LANES), lambda i: (0, 0)),
                  pl.BlockSpec(place.shape, lambda i: (0, 0, 0, 0))],
        out_specs=(pl.BlockSpec((GATE_TILE, LANES), lambda i: (i, 0)),
                   pl.BlockSpec((n_hp, GATE_TILE, LANES), lambda i: (0, i, 0))),
        scratch_shapes=[pltpu.VMEM((SUBLANES, LANES), F32)],
        compiler_params=_cparams("arbitrary"),
        name="fox_gate",
    )(p, b, jnp.asarray(place, BF16))


def _kmean_body(k_ref, o_ref):
    j = pl.program_id(1)
    o_ref[0, pl.ds(j, 1), :] = jnp.sum(k_ref[...], axis=0, keepdims=True) * (1.0 / MOBA_BLOCK)


def moba_block_means(p, k_col_block, batch, seq_len):
    nb = seq_len // MOBA_BLOCK
    return pl.pallas_call(
        _kmean_body,
        out_shape=jax.ShapeDtypeStruct((batch, nb, D_A), F32),
        grid=(batch, nb),
        in_specs=[pl.BlockSpec((MOBA_BLOCK, D_A), lambda b, j: (b * nb + j, k_col_block))],
        out_specs=pl.BlockSpec((1, nb, D_A), lambda b, j: (b, 0, 0)),
        compiler_params=_cparams("parallel", "arbitrary"),
        name="moba_block_means",
    )(p)


def _top_blocks(gate, n_valid, own, n_blocks):
    jidx = lax.broadcasted_iota(jnp.int32, gate.shape, 1).astype(F32)
    gate = jnp.where(jidx < n_valid, gate, NEG)
    sel = (jidx == own).astype(F32)
    for _ in range(MOBA_TOPK):
        mx = jnp.max(gate, axis=-1, keepdims=True)
        am = jnp.min(jnp.where(gate == mx, jidx, float(n_blocks)), axis=-1, keepdims=True)
        hit = jidx == am
        sel = jnp.where(jnp.logical_and(hit, am < n_valid), 1.0, sel)
        gate = jnp.where(hit, -jnp.inf, gate)
    return sel


def _flash_body(qi_tab, ki_tab, w_tab, q_ref, k_ref, v_ref, qx_ref, kx_ref, *rest, mode, n_hp, win):
    if mode == "moba":
        km_ref, o_ref, qm_sc, acc_sc, m_sc = rest
    elif mode == "fox":
        o_ref, qm_sc, acc_sc, m_sc = rest
    else:
        lb_ref, o_ref, qm_sc, acc_sc, m_sc = rest
    t = q_ref.shape[0]
    hp = pl.program_id(0) % n_hp
    s = pl.program_id(1)
    qi = qi_tab[s]
    ki = ki_tab[s]
    delta = qi - ki
    lane = lax.broadcasted_iota(jnp.int32, (t, LANES), 1)
    lo = lane < HEAD_DIM
    own_half = (lo, jnp.logical_not(lo))

    @pl.when(ki == jnp.maximum(qi - win, 0))
    def _():
        q = q_ref[...]
        qs = q * (SCALE * LOG2E)
        acc_sc[...] = jnp.zeros_like(acc_sc)
        m_sc[...] = jnp.full_like(m_sc, NEG)
        for h in range(2):
            qm = jnp.where(own_half[h], qs, qx_ref[0, h:h + 1, :])
            if mode == "moba":
                km = km_ref[0]
                nb = km.shape[0]
                gate = lax.dot_general(jnp.where(own_half[h], q, 0.0), km, (((1,), (1,)), ((), ())),
                                       preferred_element_type=F32, precision=HIGHEST)
                own_blk = (qi * (t // MOBA_BLOCK)
                           + lax.broadcasted_iota(jnp.int32, (t, 1), 0) // MOBA_BLOCK).astype(F32)
                hidden = (_top_blocks(gate, own_blk, own_blk, nb) - 1.0) * (-NEG)
                base = (1 - h) * HEAD_DIM + X_BLOCK
                place = (lax.broadcasted_iota(jnp.int32, (nb, LANES), 1)
                         == lax.broadcasted_iota(jnp.int32, (nb, LANES), 0) + base).astype(BF16)
                qm = qm + jnp.dot(hidden.astype(BF16), place, preferred_element_type=F32)
            qm_sc[h] = qm.astype(BF16)

    def step(causal):
        k2 = k_ref[...]
        v2 = v_ref[...]
        if mode == "fox":
            kx = kx_ref[0].astype(F32)
        else:
            kx = kx_ref[...]
        if mode == "moba":
            key_blk = ki * (t // MOBA_BLOCK) + lax.broadcasted_iota(jnp.int32, (t, LANES), 0) // MOBA_BLOCK
            kx = jnp.where((lane % HEAD_DIM) == key_blk + X_BLOCK, 1.0, kx)
        if causal:
            keep = (lax.broadcasted_iota(jnp.int32, (t, t), 1) <= lax.broadcasted_iota(jnp.int32, (t, t), 0))
        for h in range(2):
            k_aug = jnp.where(own_half[h], k2, kx).astype(BF16)
            sc = lax.dot_general(qm_sc[h], k_aug, (((1,), (1,)), ((), ())), preferred_element_type=F32)
            if mode == "dil":
                sc = sc + lb_ref[delta]
            if causal:
                sc = jnp.where(keep, sc, NEG)
            off = 0.0 if mode == "fox" else w_tab[hp * 2 + h] * (ki * t).astype(F32)
            m_old = m_sc[h]
            m_new = jnp.maximum(m_old, jnp.max(sc, axis=-1, keepdims=True) + off)
            alpha = jnp.exp2(m_old - m_new)
            p = jnp.exp2(sc - pltpu.repeat(m_new - off, t // LANES, axis=1))
            v_aug = jnp.where(own_half[h], v2, 1.0).astype(BF16)
            acc_sc[h] = alpha * acc_sc[h] + jnp.dot(p.astype(BF16), v_aug, preferred_element_type=F32)
            m_sc[h] = m_new

    if mode == "dil":
        step(False)
    else:
        pl.when(delta == 0)(lambda: step(True))
        pl.when(delta != 0)(lambda: step(False))

    @pl.when(ki == qi)
    def _():
        a0 = acc_sc[0]
        a1 = acc_sc[1]
        o_ref[...] = jnp.where(lo, a0 / a0[:, HEAD_DIM:HEAD_DIM + 1], a1 / a1[:, 0:1]).astype(o_ref.dtype)


def _log2_multiplicity(d):
    m = np.zeros(d.shape, np.float64)
    for window, dil in DILATED_BRANCHES:
        m += ((d >= 0) & (d <= window) & (d % dil == 0))
    return np.where(m > 0, np.log2(np.maximum(m, 1.0)), NEG).astype(np.float32)


def _alibi_extras(n_heads, t):
    qx = np.zeros((n_heads, LANES), np.float32)
    w_eff = np.zeros((n_heads,), np.float32)
    for hg, slope in enumerate(_alibi_slopes(n_heads)):
        base = (1 - hg % 2) * HEAD_DIM
        for k, piece in enumerate(_bf16_pieces_np(slope * LOG2E)):
            qx[hg, base + X_POS_HI + k] = POS_SPLIT * piece
            qx[hg, base + X_POS_LO + k] = piece
            w_eff[hg] += piece
    col = np.arange(t)
    kx = np.zeros((t, LANES), np.float32)
    for base in (0, HEAD_DIM):
        for k in range(N_PIECES):
            kx[:, base + X_POS_HI + k] = col // POS_SPLIT
            kx[:, base + X_POS_LO + k] = col % POS_SPLIT
    return qx.reshape(n_heads // 2, 2, LANES), kx, w_eff


def prompt_attention(p, mode, batch, seq_len, n_heads, q_blk, k_blk, v_blk, extra=()):
    t = ATT_TILE
    nq = seq_len // t
    n_hp = n_heads // 2
    win = (WIN_MAX // t) if mode == "dil" else nq
    steps = [(qi, ki) for qi in range(nq) for ki in range(max(0, qi - win), qi + 1)]
    qi_tab = jnp.asarray([s[0] for s in steps], jnp.int32)
    ki_tab = jnp.asarray([s[1] for s in steps], jnp.int32)
    assert t // POS_SPLIT <= 256 and X_BLOCK + seq_len // MOBA_BLOCK <= HEAD_DIM

    def qmap(g, s, qt, kt, wt):
        return ((g // n_hp) * nq + qt[s], q_blk + g % n_hp)

    def kmap(g, s, qt, kt, wt):
        return ((g // n_hp) * nq + kt[s], k_blk + g % n_hp)

    def vmap_(g, s, qt, kt, wt):
        return ((g // n_hp) * nq + kt[s], v_blk + g % n_hp)

    def omap(g, s, qt, kt, wt):
        return ((g // n_hp) * nq + qt[s], g % n_hp)

    if mode == "fox":
        kx, = extra
        qx = np.zeros((n_heads, LANES), np.float32)
        for hg in range(n_heads):
            qx[hg, (1 - hg % 2) * HEAD_DIM + X_FORGET:(1 - hg % 2) * HEAD_DIM + X_FORGET + N_PIECES] = 1.0
        qx = qx.reshape(n_hp, 2, LANES)
        w_eff = np.zeros((n_heads,), np.float32)
        kx_spec = pl.BlockSpec((1, t, LANES), lambda g, s, qt, kt, wt: (g % n_hp, (g // n_hp) * nq + kt[s], 0))
    else:
        qx, kx, w_eff = _alibi_extras(n_heads, t)
        kx = jnp.asarray(kx)
        kx_spec = pl.BlockSpec((t, LANES), lambda g, s, qt, kt, wt: (0, 0))
    in_specs = [pl.BlockSpec((t, LANES), qmap), pl.BlockSpec((t, LANES), kmap), pl.BlockSpec((t, LANES), vmap_),
                pl.BlockSpec((1, 2, LANES), lambda g, s, qt, kt, wt: (g % n_hp, 0, 0)), kx_spec]
    args = [p, p, p, jnp.asarray(qx), kx]
    if mode == "moba":
        km, = extra
        in_specs.append(pl.BlockSpec((1, km.shape[1], LANES), lambda g, s, qt, kt, wt: (g // n_hp, 0, g % n_hp)))
        args.append(km)
    elif mode == "dil":
        d = (np.arange(win + 1)[:, None, None] * t + np.arange(t)[None, :, None] - np.arange(t)[None, None, :])
        lb = jnp.asarray(_log2_multiplicity(d))
        in_specs.append(pl.BlockSpec(lb.shape, lambda g, s, qt, kt, wt: (0, 0, 0)))
        args.append(lb)
    return pl.pallas_call(
        functools.partial(_flash_body, mode=mode, n_hp=n_hp, win=win),
        out_shape=jax.ShapeDtypeStruct((batch * seq_len, n_heads * HEAD_DIM), BF16),
        grid_spec=pltpu.PrefetchScalarGridSpec(
            num_scalar_prefetch=3,
            grid=(batch * n_hp, len(steps)),
            in_specs=in_specs,
            out_specs=pl.BlockSpec((t, LANES), omap),
            scratch_shapes=[pltpu.VMEM((2, t, LANES), BF16), pltpu.VMEM((2, t, LANES), F32),
                            pltpu.VMEM((2, t, LANES), F32)]),
        compiler_params=_cparams("parallel", "arbitrary"),
        name="prompt_attention_" + mode,
    )(qi_tab, ki_tab, jnp.asarray(w_eff), *args)


def _sample_block_means_body(pt_ref, *refs, group, pages_per_block):
    k_refs, o_ref = refs[:group], refs[group]
    s = pl.program_id(1)

    @pl.when(s == 0)
    def _():
        o_ref[...] = jnp.zeros_like(o_ref)

    lane = lax.broadcasted_iota(jnp.int32, o_ref.shape[1:], 2)
    for g in range(group):
        blk = (s * group + g) // pages_per_block
        part = jnp.sum(k_refs[g][0], axis=-1, keepdims=True) * (1.0 / MOBA_BLOCK)
        o_ref[0] = o_ref[0] + jnp.where(lane == blk, part, 0.0)


def sample_block_means(pool_t, page_table):
    db, n_pages = page_table.shape
    _, n_heads, _, page = pool_t.shape
    group = SAMPLE_PAGE_GROUP
    return pl.pallas_call(
        functools.partial(_sample_block_means_body, group=group, pages_per_block=MOBA_BLOCK // page),
        out_shape=jax.ShapeDtypeStruct((db, n_heads, HEAD_DIM, LANES), F32),
        grid_spec=pltpu.PrefetchScalarGridSpec(
            num_scalar_prefetch=1,
            grid=(db, n_pages // group),
            in_specs=[pl.BlockSpec((1, n_heads, HEAD_DIM, page),
                                   functools.partial(lambda b, s, pt, g: (pt[b * n_pages + s * group + g], 0, 0, 0), g=g))
                      for g in range(group)],
            out_specs=pl.BlockSpec((1, n_heads, HEAD_DIM, LANES), lambda b, s, pt: (b, 0, 0, 0))),
        compiler_params=_cparams("parallel", "arbitrary"),
        name="sample_block_means",
    )(page_table.reshape(-1), *([pool_t] * group))


def _sample_body(pt_ref, q_ref, kn_ref, vn_ref, *rest, mode, n_heads, n_steps, group, page, pages_per_block, paged):
    n_pool = group if paged else 1
    kp, vp, rest = rest[:n_pool], rest[n_pool:2 * n_pool], rest[2 * n_pool:]
    if mode == "moba":
        bias_ref, biasn_ref, kms_ref, o_ref, qbd_sc, kn_sc, vn_sc, acc_sc, m_sc, l_sc, sel_sc = rest
    elif mode == "fox":
        lfp, rest = rest[:group], rest[group:]
        lfn_ref, o_ref, qbd_sc, kn_sc, vn_sc, acc_sc, m_sc, l_sc, ncb_sc, run_sc = rest
    else:
        bias_ref, biasn_ref, mult_ref, multn_ref, o_ref, qbd_sc, kn_sc, vn_sc, acc_sc, m_sc, l_sc = rest
    s = pl.program_id(1)
    nq = q_ref.shape[0]
    rows, width = acc_sc.shape
    lane = lax.broadcasted_iota(jnp.int32, (rows, LANES), 1)
    row_q = lax.broadcasted_iota(jnp.int32, (rows, LANES), 0) % nq
    col_head = lax.broadcasted_iota(jnp.int32, (nq, width), 1) // HEAD_DIM

    def attend(scores, pv, bias, ok, mult):
        logit = scores + bias
        if ok is not None:
            logit = jnp.where(ok, logit, NEG)
        m_old = m_sc[...]
        m_new = jnp.maximum(m_old, jnp.max(logit, axis=-1, keepdims=True))
        alpha = jnp.exp(m_old - m_new)
        p = jnp.exp(logit - pltpu.repeat(m_new, logit.shape[1] // LANES, axis=1))
        if mult is not None:
            p = p * mult
        l_sc[...] = alpha * l_sc[...] + jnp.sum(p, axis=-1, keepdims=True)
        acc_sc[...] = pltpu.repeat(alpha, width // LANES, axis=1) * acc_sc[...] + pv(p.astype(BF16))
        m_sc[...] = m_new

    @pl.when(s == 0)
    def _():
        acc_sc[...] = jnp.zeros_like(acc_sc)
        l_sc[...] = jnp.zeros_like(l_sc)
        m_sc[...] = jnp.full_like(m_sc, NEG)
        kn_sc[...] = jnp.zeros_like(kn_sc)
        vn_sc[...] = jnp.zeros_like(vn_sc)
        kn_sc[0:nq, :] = kn_ref[...].astype(BF16)
        vn_sc[0:nq, :] = vn_ref[...].astype(BF16)
        q = q_ref[...]
        for h in range(n_heads):
            qbd_sc[h * nq:(h + 1) * nq, :] = jnp.where(col_head == h, q * SCALE, 0.0).astype(BF16)
        causal = lane <= row_q
        scores = lax.dot_general(qbd_sc[...], kn_sc[...], (((1,), (1,)), ((), ())), preferred_element_type=F32)
        pv_new = lambda p: jnp.dot(p, vn_sc[...], preferred_element_type=F32)
        if mode == "moba":
            gate = jnp.concatenate(
                [jnp.dot(q[:, h * HEAD_DIM:(h + 1) * HEAD_DIM], kms_ref[0, h],
                         preferred_element_type=F32, precision=HIGHEST) for h in range(n_heads)], axis=0)
            nb = n_steps * group // pages_per_block
            sel_sc[...] = _top_blocks(gate, float(nb), float(nb), LANES)
            attend(scores, pv_new, biasn_ref[...], causal, None)
        elif mode == "fox":
            lfn = lfn_ref[...]
            tri = (lax.broadcasted_iota(jnp.int32, (nq, nq), 1)
                   <= lax.broadcasted_iota(jnp.int32, (nq, nq), 0)).astype(F32)
            newcum = jnp.dot(tri, lfn, preferred_element_type=F32, precision=HIGHEST)
            eye = lane[:nq] == row_q[:nq]
            ncb, nrow = [], []
            for h in range(n_heads):
                col = jnp.broadcast_to(newcum[:, h:h + 1], (nq, LANES))
                ncb.append(col)
                nrow.append(jnp.broadcast_to(jnp.sum(jnp.where(eye, col, 0.0), axis=0, keepdims=True), (nq, LANES)))
            ncb = jnp.concatenate(ncb, axis=0)
            ncb_sc[...] = ncb
            run_sc[...] = jnp.zeros_like(run_sc)
            attend(scores, pv_new, ncb - jnp.concatenate(nrow, axis=0), causal, None)
        else:
            multn = multn_ref[...]
            attend(scores, pv_new, biasn_ref[...], multn > 0.0, multn)

    if paged:
        k_t = jnp.concatenate([kp[g][0].reshape(width, page) for g in range(group)], axis=1).astype(BF16)
        v_t = jnp.concatenate([vp[g][0].reshape(width, page) for g in range(group)], axis=1).astype(BF16)
    else:
        k_t = kp[0][0].reshape(width, group * page).astype(BF16)
        v_t = vp[0][0].reshape(width, group * page).astype(BF16)
    scores = jnp.dot(qbd_sc[...], k_t, preferred_element_type=F32)
    pv = lambda p: lax.dot_general(p, v_t, (((1,), (1,)), ((), ())), preferred_element_type=F32)
    if mode == "moba":
        hidden = []
        for g in range(group):
            blk = (s * group + g) // pages_per_block
            vis = jnp.max(jnp.where(lane == blk, sel_sc[...], 0.0), axis=-1, keepdims=True)
            hidden.append(jnp.broadcast_to(jnp.where(vis > 0.5, 0.0, NEG), (rows, page)))
        attend(scores, pv, bias_ref[0] + jnp.concatenate(hidden, axis=1), None, None)
    elif mode == "fox":
        lft = jnp.concatenate([lfp[g][0] for g in range(group)], axis=0)
        upper = (lax.broadcasted_iota(jnp.int32, (page, page), 0)
                 > lax.broadcasted_iota(jnp.int32, (page, page), 1)).astype(F32)
        inside = jnp.dot(lft, upper, preferred_element_type=F32, precision=HIGHEST)
        totals = jnp.sum(lft, axis=-1, keepdims=True)
        run = run_sc[...]
        pieces = []
        for g in range(group):
            after = inside[g * n_heads:(g + 1) * n_heads] + run
            pieces.append(jnp.concatenate(
                [jnp.broadcast_to(after[h:h + 1, :], (nq, page)) for h in range(n_heads)], axis=0))
            run = run + totals[g * n_heads:(g + 1) * n_heads]
        run_sc[...] = run
        attend(scores, pv, pltpu.repeat(ncb_sc[...], group, axis=1) + jnp.concatenate(pieces, axis=1), None, None)
    else:
        mult = mult_ref[0]
        attend(scores, pv, bias_ref[0], mult > 0.0, mult)

    @pl.when(s == n_steps - 1)
    def _():
        out = acc_sc[...] / pltpu.repeat(l_sc[...], width // LANES, axis=1)
        o_ref[...] = sum(jnp.where(col_head == h, out[h * nq:(h + 1) * nq, :], 0.0) for h in range(n_heads))


def _multiplicity(d):
    m = np.zeros(d.shape, np.float32)
    for window, dil in DILATED_BRANCHES:
        m += ((d >= 0) & (d <= window) & (d % dil == 0)).astype(np.float32)
    return m


def _sample_tables(past_len, nq, n_heads, n_steps, span):
    d = (past_len + np.arange(nq)[None, :, None]) - (np.arange(n_steps)[:, None, None] * span
                                                     + np.arange(span)[None, None, :])
    dn = np.arange(nq)[:, None] - np.arange(LANES)[None, :]
    slopes = np.repeat(np.asarray(_alibi_slopes(n_heads), np.float64), nq)[None, :, None]
    tile = lambda a: np.tile(a, (1, n_heads, 1))
    d, dn = tile(d), tile(dn[None])
    return ((-slopes * d).astype(np.float32), (-slopes * dn)[0].astype(np.float32),
            _multiplicity(d), _multiplicity(dn)[0])


def sample_attention(p, mode, row0, db, nq, n_heads, q_blk, k_blk, v_blk, k_pool_t, v_pool_t, page_table, extra=()):
    width = n_heads * HEAD_DIM
    rows = n_heads * nq
    group = SAMPLE_PAGE_GROUP
    reverse = mode == "fox"
    paged = page_table is not None
    page = k_pool_t.shape[3] if paged else LANES
    assert page == LANES
    n_pages = page_table.shape[1] if paged else k_pool_t.shape[3] // page
    pt = page_table.reshape(-1) if paged else jnp.zeros((1,), jnp.int32)
    n_steps = n_pages // group
    span = group * page
    rb0 = row0 // nq

    def pg(s, g):
        j = s * group + g
        return (n_pages - 1 - j) if reverse else j

    def pool_spec(g):
        return pl.BlockSpec((1, n_heads, HEAD_DIM, page), lambda b, s, t: (t[b * n_pages + pg(s, g)], 0, 0, 0))

    if paged:
        pool_specs = [pool_spec(g) for g in range(group)]
        pools = lambda a: [a] * group
    else:
        pool_specs = [pl.BlockSpec((1, n_heads, HEAD_DIM, span), lambda b, s, t: (b, 0, 0, s))]
        pools = lambda a: [a]
    bias, biasn, mult, multn = [jnp.asarray(a) for a in _sample_tables(n_pages * page, nq, n_heads, n_steps, span)]
    table_spec = pl.BlockSpec((1, rows, span), lambda b, s, t: (s, 0, 0))
    new_spec = pl.BlockSpec((rows, LANES), lambda b, s, t: (0, 0))
    in_specs = [pl.BlockSpec((nq, width), lambda b, s, t: (rb0 + b, q_blk)),
                pl.BlockSpec((nq, width), lambda b, s, t: (rb0 + b, k_blk)),
                pl.BlockSpec((nq, width), lambda b, s, t: (rb0 + b, v_blk))]
    in_specs += pool_specs * 2
    args = [p, p, p] + pools(k_pool_t) + pools(v_pool_t)
    scratch = [pltpu.VMEM((rows, width), BF16), pltpu.VMEM((LANES, width), BF16), pltpu.VMEM((LANES, width), BF16),
               pltpu.VMEM((rows, width), F32), pltpu.VMEM((rows, LANES), F32), pltpu.VMEM((rows, LANES), F32)]
    if mode == "moba":
        kms, = extra
        in_specs += [table_spec, new_spec, pl.BlockSpec((1,) + kms.shape[1:], lambda b, s, t: (b, 0, 0, 0))]
        args += [bias, biasn, kms]
        scratch.append(pltpu.VMEM((rows, LANES), F32))
    elif mode == "fox":
        lf_pool_t, logf = extra
        in_specs += [pl.BlockSpec((1, n_heads, page),
                                  functools.partial(lambda b, s, t, g: (t[b * n_pages + pg(s, g)], 0, 0), g=g))
                     for g in range(group)]
        in_specs.append(pl.BlockSpec((nq, LANES), lambda b, s, t: (rb0 + b, 0)))
        args += [lf_pool_t] * group + [logf]
        scratch += [pltpu.VMEM((rows, LANES), F32), pltpu.VMEM((n_heads, LANES), F32)]
    else:
        in_specs += [table_spec, new_spec, table_spec, new_spec]
        args += [bias, biasn, mult, multn]
    return pl.pallas_call(
        functools.partial(_sample_body, mode=mode, n_heads=n_heads, n_steps=n_steps, group=group, page=page,
                          pages_per_block=MOBA_BLOCK // page, paged=paged),
        out_shape=jax.ShapeDtypeStruct((db * nq, width), F32),
        grid_spec=pltpu.PrefetchScalarGridSpec(
            num_scalar_prefetch=1,
            grid=(db, n_steps),
            in_specs=in_specs,
            out_specs=pl.BlockSpec((nq, width), lambda b, s, t: (b, 0)),
            scratch_shapes=scratch),
        compiler_params=_cparams("parallel", "arbitrary"),
        name="sample_attention_" + mode,
    )(pt, *args)


RANK_NONE = 127.0


def _top_rows(x, k, with_rank=False):
    vals = []
    rank = jnp.full(x.shape, RANK_NONE, F32)
    for r in range(k):
        m = jnp.max(x, axis=0, keepdims=True)
        vals.append(m)
        hit = x == m
        if with_rank:
            rank = jnp.where(hit, float(r), rank)
        x = jnp.where(hit, -jnp.inf, x)
    return (vals, rank) if with_rank else vals


def _peer_scores_body(ht_ref, wq_ref, sk_ref, n1_ref, r2_ref, e1_ref, e2_ref):
    half = PEER_KEYS
    k1 = PEER_TOPK + 1
    qt = jnp.dot(wq_ref[...], ht_ref[...], preferred_element_type=F32)
    s1 = jnp.dot(sk_ref[0, 0], qt[:half], preferred_element_type=F32, precision=HIGHEST)
    s2 = jnp.dot(sk_ref[0, 1], qt[half:], preferred_element_type=F32, precision=HIGHEST)
    t1 = _top_rows(s1, k1)
    t2, rank2 = _top_rows(s2, k1, with_rank=True)
    cand = [t1[i] + t2[j] for i in range(k1) for j in range(k1) if (i + 1) * (j + 1) <= k1]
    pad = -len(cand) % SUBLANES
    cand = jnp.concatenate(cand + [jnp.full_like(t1[0], -jnp.inf)] * pad, axis=0)
    top = _top_rows(cand, k1)
    z = sum(jnp.exp(v - top[0]) for v in top[:PEER_TOPK])
    tau = 0.5 * (top[PEER_TOPK - 1] + top[PEER_TOPK])
    need = tau - s1
    n1 = jnp.zeros_like(s1)
    for j in range(k1):
        n1 = jnp.where(t2[j] >= need, float(j + 1), n1)
    n1_ref[0] = n1
    r2_ref[0] = pltpu.bitcast(rank2.astype(BF16), jnp.uint32)
    e1_ref[0] = jnp.exp(s1 - t1[0]) / z
    e2_ref[0] = pltpu.bitcast(jnp.exp(s2 - t2[0]).astype(BF16), jnp.uint32)


def peer_scores(ht, wq_t, subkeys):
    d, t = ht.shape
    tt = PEER_TOKEN_TILE
    nh = PEER_HEADS
    dq = wq_t.shape[0] // nh
    big = jax.ShapeDtypeStruct((nh, PEER_KEYS, t), F32)
    small = jax.ShapeDtypeStruct((nh, PEER_KEYS // 2, t), jnp.uint32)
    bspec = pl.BlockSpec((1, PEER_KEYS, tt), lambda i, h: (h, 0, i))
    pspec = pl.BlockSpec((1, PEER_KEYS // 2, tt), lambda i, h: (h, 0, i))
    return pl.pallas_call(
        _peer_scores_body,
        out_shape=(big, small, big, small),
        grid=(t // tt, nh),
        in_specs=[pl.BlockSpec((d, tt), lambda i, h: (0, i)),
                  pl.BlockSpec((dq, d), lambda i, h: (h, 0)),
                  pl.BlockSpec((1,) + subkeys.shape[1:], lambda i, h: (h, 0, 0, 0))],
        out_specs=(bspec, pspec, bspec, pspec),
        compiler_params=_cparams("parallel", "arbitrary"),
        name="peer_scores",
    )(ht, wq_t, subkeys)


def _gelu(x):
    return 0.5 * x * (1.0 + lax.erf(x * (1.0 / math.sqrt(2.0))))


def _peer_experts_body(ht_ref, u_ref, vt_ref, n1_ref, r2_ref, e1_ref, e2_ref, x_ref, g_ref,
                       xn_ref, hn_ref, acc_sc, act_sc, w_sc):
    j = pl.program_id(1)
    te = u_ref.shape[0]
    tt = ht_ref.shape[1]
    groups = te // PEER_KEYS

    @pl.when(j == 0)
    def _():
        acc_sc[...] = jnp.zeros_like(acc_sc)

    act_sc[...] = _gelu(jnp.dot(u_ref[...], ht_ref[...], preferred_element_type=F32))
    for al in range(groups):
        a = j * groups + al
        rows = slice(al * PEER_KEYS, (al + 1) * PEER_KEYS)
        n1 = [n1_ref[h, pl.ds(a, 1), :].astype(BF16) for h in range(PEER_HEADS)]
        e1 = [e1_ref[h, pl.ds(a, 1), :].astype(BF16) for h in range(PEER_HEADS)]
        for tc in range(tt // LANES):
            cols = slice(tc * LANES, (tc + 1) * LANES)
            coef = jnp.zeros((PEER_KEYS, LANES), BF16)
            for h in range(PEER_HEADS):
                hit = pltpu.bitcast(r2_ref[h, :, cols], BF16) < n1[h][:, cols]
                e2 = pltpu.bitcast(e2_ref[h, :, cols], BF16)
                coef = coef + jnp.where(hit, e2, jnp.zeros((), BF16)) * e1[h][:, cols]
            w_sc[rows, cols] = (coef.astype(F32) * act_sc[rows, cols]).astype(BF16)
    acc_sc[...] += jnp.dot(vt_ref[...], w_sc[...], preferred_element_type=F32)

    @pl.when(j == pl.num_programs(1) - 1)
    def _():
        xn = x_ref[...] + acc_sc[...].T
        xn_ref[...] = xn
        hn_ref[...] = _rms(xn, g_ref[...]).astype(hn_ref.dtype)


def peer_experts(x, ht, u, v_t, scores, g_next, norm_dtype):
    t, d = x.shape
    tt = PEER_TOKEN_TILE
    te = PEER_EXPERT_TILE
    n_exp = u.shape[0]
    sspec = pl.BlockSpec((PEER_HEADS, PEER_KEYS, tt), lambda i, j: (0, 0, i))
    pspec = pl.BlockSpec((PEER_HEADS, PEER_KEYS // 2, tt), lambda i, j: (0, 0, i))
    return pl.pallas_call(
        _peer_experts_body,
        out_shape=(jax.ShapeDtypeStruct((t, d), F32), jax.ShapeDtypeStruct((t, d), norm_dtype)),
        grid=(t // tt, n_exp // te),
        in_specs=[pl.BlockSpec((d, tt), lambda i, j: (0, i)),
                  pl.BlockSpec((te, d), lambda i, j: (j, 0)),
                  pl.BlockSpec((d, te), lambda i, j: (0, j)),
                  sspec, pspec, sspec, pspec,
                  pl.BlockSpec((tt, d), lambda i, j: (i, 0)),
                  pl.BlockSpec((1, d), lambda i, j: (0, 0))],
        out_specs=(pl.BlockSpec((tt, d), lambda i, j: (i, 0)),
                   pl.BlockSpec((tt, d), lambda i, j: (i, 0))),
        scratch_shapes=[pltpu.VMEM((d, tt), F32), pltpu.VMEM((te, tt), F32), pltpu.VMEM((te, tt), BF16)],
        compiler_params=_cparams("parallel", "arbitrary"),
        name="peer_experts",
    )(ht, u, v_t, *scores, x, g_next.reshape(1, d))


def _heads(a, n):
    return a.reshape(a.shape[:-1] + (n, HEAD_DIM))


def kernel(x_prompt, x_sample, cache_moba_k, cache_moba_v, cache_fox_k, cache_fox_v, cache_fox_logf,
           state_swa_k, state_swa_v, page_table, norm_gain, final_gain, w_in_even, b_forget, w_out_even,
           w_in_odd, w_out_odd, peer_w_query, peer_subkeys, peer_u, peer_v):
    b, s, d = x_prompt.shape
    db, ds, _ = x_sample.shape
    tp = b * s
    depth = norm_gain.shape[0]
    n_pool, page = cache_moba_k.shape[1:3]
    past_len = page_table.shape[1] * page
    win_buf = state_swa_k.shape[2]
    keep_p = min(WIN_MAX, s)
    assert win_buf == past_len and past_len % MOBA_BLOCK == 0 and MOBA_BLOCK % page == 0
    assert s % ATT_TILE == 0 and (tp + db * ds) % ROW_TILE == 0 and tp % ds == 0

    x = jnp.concatenate([x_prompt.reshape(tp, d), x_sample.reshape(db * ds, d)], axis=0)
    h = rmsnorm_rows(x, norm_gain[0, 0], BF16)
    new = {}
    for layer in range(depth):
        li = layer // 2
        if layer % 2 == 0:
            n_in = w_in_even.shape[2]
            n_pad = -n_in % (5 * LANES)
            w_in = jnp.pad(w_in_even[li], ((0, 0), (0, n_pad))).astype(BF16)
            p = matmul(h, w_in, (n_in + n_pad) // 5)
            logf, fox_kx = fox_gate(p, 3 * (D_A + D_B) // LANES, b_forget[li], s)
            km = moba_block_means(p, 1, b, s)
            oa_p = prompt_attention(p, "moba", b, s, N_HEADS_A, 0, D_A // LANES, 2 * D_A // LANES, (km,))
            fb = 3 * D_A // LANES
            ob_p = prompt_attention(p, "fox", b, s, N_HEADS_B, fb, fb + D_B // LANES, fb + 2 * D_B // LANES,
                                    (fox_kx,))
            pool = lambda c: jnp.transpose(c[li], (0, 2, 3, 1))
            ka_pool, va_pool = pool(cache_moba_k), pool(cache_moba_v)
            kb_pool, vb_pool = pool(cache_fox_k), pool(cache_fox_v)
            kms = sample_block_means(ka_pool, page_table)
            oa_s = sample_attention(p, "moba", tp, db, ds, N_HEADS_A, 0, 1, 2, ka_pool, va_pool, page_table, (kms,))
            lf_pool_t = jnp.swapaxes(cache_fox_logf[li], 1, 2)
            ob_s = sample_attention(p, "fox", tp, db, ds, N_HEADS_B, 3, 4, 5, kb_pool, vb_pool, page_table,
                                    (lf_pool_t, logf))
            w_out = w_out_even[li].astype(BF16)
            pairs = [(jnp.concatenate([oa_p, oa_s.astype(BF16)], axis=0), w_out[:D_A]),
                     (jnp.concatenate([ob_p, ob_s.astype(BF16)], axis=0), w_out[D_A:])]
            cuts = {"mk": (D_A, N_HEADS_A), "mv": (2 * D_A, N_HEADS_A), "fk": (3 * D_A + D_B, N_HEADS_B),
                    "fv": (3 * D_A + 2 * D_B, N_HEADS_B)}
            for name, (c0, nh) in cuts.items():
                cols = p[:, c0:c0 + nh * HEAD_DIM]
                new.setdefault("p_" + name, []).append(_heads(cols[:tp].reshape(b, s, -1), nh))
                new.setdefault("s_" + name, []).append(_heads(cols[tp:].reshape(db, ds, -1), nh))
            new.setdefault("p_fl", []).append(logf[:tp, :N_HEADS_B].reshape(b, s, N_HEADS_B))
            new.setdefault("s_fl", []).append(logf[tp:, :N_HEADS_B].reshape(db, ds, N_HEADS_B))
        else:
            p = matmul(h, w_in_odd[li].astype(BF16), D_C // 2)
            nb = D_C // LANES
            oc_p = prompt_attention(p, "dil", b, s, N_HEADS_C, 0, nb, 2 * nb)
            kc_pool = jnp.transpose(state_swa_k[li], (0, 2, 3, 1))
            vc_pool = jnp.transpose(state_swa_v[li], (0, 2, 3, 1))
            oc_s = sample_attention(p, "dil", tp, db, ds, N_HEADS_C, 0, 1, 2, kc_pool, vc_pool, None)
            pairs = [(jnp.concatenate([oc_p, oc_s.astype(BF16)], axis=0), w_out_odd[li].astype(BF16))]
            for name, c0 in (("sk", D_C), ("sv", 2 * D_C)):
                cols = p[:, c0:c0 + D_C]
                new.setdefault("p_" + name, []).append(
                    _heads(cols[:tp].reshape(b, s, -1)[:, s - keep_p:], N_HEADS_C))
                new.setdefault("s_" + name, []).append(_heads(cols[tp:].reshape(db, ds, -1), N_HEADS_C))
        x, _, h_t = resid_norm(x, pairs, norm_gain[layer, 1])
        scores = peer_scores(h_t, peer_w_query[layer].T.astype(BF16), peer_subkeys[layer])
        last = layer == depth - 1
        g_next = final_gain if last else norm_gain[layer + 1, 0]
        x, h = peer_experts(x, h_t, peer_u[layer].astype(BF16), peer_v[layer].T.astype(BF16), scores,
                            g_next, F32 if last else BF16)
    y = h
    order = ["p_mk", "p_mv", "p_fk", "p_fv", "p_fl", "p_sk", "p_sv",
             "s_mk", "s_mv", "s_fk", "s_fv", "s_fl", "s_sk", "s_sv"]
    return (y[:tp].reshape(b, s, d), y[tp:].reshape(db, ds, d)) + tuple(jnp.stack(new[k]) for k in order)
```

```python
import functools
import math

import numpy as np
import jax
import jax.numpy as jnp
from jax import lax
from jax.experimental import pallas as pl
from jax.experimental.pallas import tpu as pltpu

F32 = jnp.float32
BF16 = jnp.bfloat16
HIGHEST = lax.Precision.HIGHEST

LANES = 128
SUBLANES = 8
VMEM_LIMIT_BYTES = 56 * 1024 * 1024

HEAD_DIM = 64
N_HEADS_A = 8
N_HEADS_B = 8
N_HEADS_C = 16
D_A = N_HEADS_A * HEAD_DIM
D_B = N_HEADS_B * HEAD_DIM
D_C = N_HEADS_C * HEAD_DIM
MOBA_BLOCK = 256
MOBA_TOPK = 3
DILATED_BRANCHES = ((128, 1), (512, 4), (2048, 16))
WIN_MAX = 2048
PEER_HEADS = 8
PEER_KEYS = 128
PEER_TOPK = 16
RMS_EPS = 1e-6
NEG = -1e30
SCALE = HEAD_DIM ** -0.5
LOG2E = 1.0 / math.log(2.0)

ATT_TILE = 512
ATT_HEAD_PAIRS = 2
GATE_TILE = 256
ROW_TILE = 512
PEER_TOKEN_TILE = 512
PEER_EXPERT_TILE = 1024
SAMPLE_PAGE_GROUP = 8

X_POS_HI = 0
X_POS_LO = 3
X_FORGET = 0
X_BLOCK = 8
POS_SPLIT = 16
N_PIECES = 3


def _cparams(*sem):
    return pltpu.CompilerParams(dimension_semantics=sem, vmem_limit_bytes=VMEM_LIMIT_BYTES)


def _alibi_slopes(n):
    return [2.0 ** (-8.0 * (i + 1) / n) for i in range(n)]


def _bf16_pieces_np(x):
    x = np.asarray(x, np.float32)
    out = []
    for _ in range(N_PIECES):
        p = np.asarray(np.asarray(x, dtype=BF16), np.float32)
        out.append(p)
        x = x - p
    return out


def _rms(x, g):
    return x * lax.rsqrt(jnp.mean(x * x, axis=-1, keepdims=True) + RMS_EPS) * g


def _lane_tile(x, n):
    return x if n == 1 else jnp.concatenate([x] * n, axis=1)


def _rms_body(x_ref, g_ref, h_ref):
    h_ref[...] = _rms(x_ref[...], g_ref[...]).astype(h_ref.dtype)


def rmsnorm_rows(x, g, out_dtype):
    t, d = x.shape
    return pl.pallas_call(
        _rms_body,
        out_shape=jax.ShapeDtypeStruct((t, d), out_dtype),
        grid=(t // ROW_TILE,),
        in_specs=[pl.BlockSpec((ROW_TILE, d), lambda i: (i, 0)),
                  pl.BlockSpec((1, d), lambda i: (0, 0))],
        out_specs=pl.BlockSpec((ROW_TILE, d), lambda i: (i, 0)),
        compiler_params=_cparams("parallel"),
        name="rmsnorm_rows",
    )(x, g.reshape(1, d))


def _mm_body(h_ref, w_ref, o_ref):
    o_ref[...] = jnp.dot(h_ref[...], w_ref[...], preferred_element_type=F32)


def matmul(h, w, tn):
    t, k = h.shape
    n = w.shape[1]
    return pl.pallas_call(
        _mm_body,
        out_shape=jax.ShapeDtypeStruct((t, n), F32),
        grid=(t // ROW_TILE, n // tn),
        in_specs=[pl.BlockSpec((ROW_TILE, k), lambda i, j: (i, 0)),
                  pl.BlockSpec((k, tn), lambda i, j: (0, j))],
        out_specs=pl.BlockSpec((ROW_TILE, tn), lambda i, j: (i, j)),
        compiler_params=_cparams("parallel", "arbitrary"),
        name="matmul",
    )(h, w)


def _resid_norm_body(*refs, n_pairs):
    x_ref = refs[0]
    g_ref = refs[1 + 2 * n_pairs]
    xn_ref, h_ref, ht_ref = refs[2 + 2 * n_pairs:]
    acc = x_ref[...]
    for p in range(n_pairs):
        acc = acc + jnp.dot(refs[1 + 2 * p][...], refs[2 + 2 * p][...], preferred_element_type=F32)
    xn_ref[...] = acc
    h = _rms(acc, g_ref[...])
    h_ref[...] = h.astype(h_ref.dtype)
    ht_ref[...] = h.T.astype(ht_ref.dtype)


def resid_norm(x, pairs, g):
    t, d = x.shape
    args = [x]
    in_specs = [pl.BlockSpec((ROW_TILE, d), lambda i: (i, 0))]
    for a, w in pairs:
        args += [a, w]
        in_specs += [pl.BlockSpec((ROW_TILE, a.shape[1]), lambda i: (i, 0)),
                     pl.BlockSpec(w.shape, lambda i: (0, 0))]
    args.append(g.reshape(1, d))
    in_specs.append(pl.BlockSpec((1, d), lambda i: (0, 0)))
    return pl.pallas_call(
        functools.partial(_resid_norm_body, n_pairs=len(pairs)),
        out_shape=(jax.ShapeDtypeStruct((t, d), F32),
                   jax.ShapeDtypeStruct((t, d), BF16),
                   jax.ShapeDtypeStruct((d, t), BF16)),
        grid=(t // ROW_TILE,),
        in_specs=in_specs,
        out_specs=(pl.BlockSpec((ROW_TILE, d), lambda i: (i, 0)),
                   pl.BlockSpec((ROW_TILE, d), lambda i: (i, 0)),
                   pl.BlockSpec((d, ROW_TILE), lambda i: (0, i))),
        compiler_params=_cparams("parallel"),
        name="resid_norm",
    )(*args)


def _log_sigmoid(x):
    return -(jnp.maximum(-x, 0.0) + jnp.log1p(jnp.exp(-jnp.abs(x))))


def _fox_gate_body(p_ref, b_ref, place_ref, lf_ref, kx_ref, carry_sc, *, tiles_per_seq):
    i = pl.program_id(0)
    t = p_ref.shape[0]

    @pl.when(i % tiles_per_seq == 0)
    def _():
        carry_sc[...] = jnp.zeros_like(carry_sc)

    lf = _log_sigmoid(p_ref[...] + b_ref[...])
    lf_ref[...] = lf
    row = lax.broadcasted_iota(jnp.int32, (t, t), 0)
    col = lax.broadcasted_iota(jnp.int32, (t, t), 1)
    tri = (col <= row).astype(F32)
    c = jnp.dot(tri, lf, preferred_element_type=F32, precision=HIGHEST) + carry_sc[0:1, :]
    carry_sc[...] = jnp.broadcast_to(c[t - 1:t, :], carry_sc.shape)
    rest = c * LOG2E
    pieces = []
    for _ in range(N_PIECES):
        piece = rest.astype(BF16)
        pieces.append(piece)
        rest = rest - piece.astype(F32)
    for hp in range(kx_ref.shape[0]):
        moved = sum(jnp.dot(pieces[k], place_ref[hp, k], preferred_element_type=F32) for k in range(N_PIECES))
        kx_ref[hp] = (-moved).astype(BF16)


def fox_gate(p, col_block, b_forget, seq_len):
    t = p.shape[0]
    n_heads = b_forget.shape[0]
    n_hp = n_heads // 2
    b = jnp.zeros((1, LANES), F32).at[0, :n_heads].set(b_forget)
    place = np.zeros((n_hp, N_PIECES, LANES, LANES), np.float32)
    for hp in range(n_hp):
        for k in range(N_PIECES):
            place[hp, k, 2 * hp, HEAD_DIM + X_FORGET + k] = 1.0
            place[hp, k, 2 * hp + 1, X_FORGET + k] = 1.0
    return pl.pallas_call(
        functools.partial(_fox_gate_body, tiles_per_seq=seq_len // GATE_TILE),
        out_shape=(jax.ShapeDtypeStruct((t, LANES), F32),
                   jax.ShapeDtypeStruct((n_hp, t, LANES), BF16)),
        grid=(t // GATE_TILE,),
        in_specs=[pl.BlockSpec((GATE_TILE, LANES), lambda i: (i, col_block)),
                  pl.BlockSpec((1, LANES), lambda i: (0, 0)),
                  pl.BlockSpec(place.shape, lambda i: (0, 0, 0, 0))],
        out_specs=(pl.BlockSpec((GATE_TILE, LANES), lambda i: (i, 0)),
                   pl.BlockSpec((n_hp, GATE_TILE, LANES), lambda i: (0, i, 0))),
        scratch_shapes=[pltpu.VMEM((SUBLANES, LANES), F32)],
        compiler_params=_cparams("arbitrary"),
        name="fox_gate",
    )(p, b, jnp.asarray(place, BF16))


def _kmean_body(k_ref, o_ref):
    j = pl.program_id(1)
    o_ref[0, pl.ds(j, 1), :] = jnp.sum(k_ref[...], axis=0, keepdims=True) * (1.0 / MOBA_BLOCK)


def moba_block_means(p, k_col_block, batch, seq_len):
    nb = seq_len // MOBA_BLOCK
    return pl.pallas_call(
        _kmean_body,
        out_shape=jax.ShapeDtypeStruct((batch, nb, D_A), F32),
        grid=(batch, nb),
        in_specs=[pl.BlockSpec((MOBA_BLOCK, D_A), lambda b, j: (b * nb + j, k_col_block))],
        out_specs=pl.BlockSpec((1, nb, D_A), lambda b, j: (b, 0, 0)),
        compiler_params=_cparams("parallel", "arbitrary"),
        name="moba_block_means",
    )(p)


def _top_blocks(gate, n_valid, own, n_blocks, axis=1):
    jidx = lax.broadcasted_iota(jnp.int32, gate.shape, axis).astype(F32)
    gate = jnp.where(jidx < n_valid, gate, NEG)
    sel = (jidx == own).astype(F32)
    for _ in range(MOBA_TOPK):
        mx = jnp.max(gate, axis=axis, keepdims=True)
        am = jnp.min(jnp.where(gate == mx, jidx, float(n_blocks)), axis=axis, keepdims=True)
        hit = jidx == am
        sel = jnp.where(jnp.logical_and(hit, am < n_valid), 1.0, sel)
        gate = jnp.where(hit, -jnp.inf, gate)
    return sel


def _flash_body(qi_tab, ki_tab, w_tab, q_ref, k_ref, v_ref, qx_ref, kx_ref, *rest, mode, n_hp, win):
    if mode == "moba":
        km_ref, o_ref, qm_sc, acc_sc, m_sc = rest
    elif mode == "fox":
        o_ref, qm_sc, acc_sc, m_sc = rest
    else:
        lb_ref, o_ref, qm_sc, acc_sc, m_sc = rest
    t = q_ref.shape[0]
    pairs = q_ref.shape[1] // LANES
    hp0 = (pl.program_id(0) % (n_hp // pairs)) * pairs
    s = pl.program_id(1)
    qi = qi_tab[s]
    ki = ki_tab[s]
    delta = qi - ki
    lane = lax.broadcasted_iota(jnp.int32, (t, LANES), 1)
    lo = lane < HEAD_DIM
    own_half = (lo, jnp.logical_not(lo))
    pair_lanes = [slice(pp * LANES, (pp + 1) * LANES) for pp in range(pairs)]

    @pl.when(ki == jnp.maximum(qi - win, 0))
    def _():
        acc_sc[...] = jnp.zeros_like(acc_sc)
        m_sc[...] = jnp.full_like(m_sc, NEG)
        for pp in range(pairs):
            q = q_ref[:, pair_lanes[pp]]
            qs = q * (SCALE * LOG2E)
            for h in range(2):
                qm = jnp.where(own_half[h], qs, qx_ref[pp, h:h + 1, :])
                if mode == "moba":
                    km = km_ref[0, :, pair_lanes[pp]]
                    nb = km.shape[0]
                    gate = lax.dot_general(km, jnp.where(own_half[h], q, 0.0), (((1,), (1,)), ((), ())),
                                           preferred_element_type=F32, precision=HIGHEST)
                    own_blk = (qi * (t // MOBA_BLOCK)
                               + lax.broadcasted_iota(jnp.int32, (1, t), 1) // MOBA_BLOCK).astype(F32)
                    hidden = (_top_blocks(gate, own_blk, own_blk, nb, axis=0) - 1.0) * (-NEG)
                    base = (1 - h) * HEAD_DIM + X_BLOCK
                    place = (lax.broadcasted_iota(jnp.int32, (nb, LANES), 1)
                             == lax.broadcasted_iota(jnp.int32, (nb, LANES), 0) + base).astype(BF16)
                    qm = qm + lax.dot_general(hidden.astype(BF16), place, (((0,), (0,)), ((), ())),
                                              preferred_element_type=F32)
                qm_sc[pp * 2 + h] = qm.astype(BF16)

    def step(causal):
        if mode != "fox":
            kx = kx_ref[...]
        if mode == "moba":
            key_blk = ki * (t // MOBA_BLOCK) + lax.broadcasted_iota(jnp.int32, (t, LANES), 0) // MOBA_BLOCK
            kx = jnp.where((lane % HEAD_DIM) == key_blk + X_BLOCK, 1.0, kx)
        if causal:
            keep = (lax.broadcasted_iota(jnp.int32, (t, t), 1) <= lax.broadcasted_iota(jnp.int32, (t, t), 0))
        for pp in range(pairs):
            k2 = k_ref[:, pair_lanes[pp]]
            v2 = v_ref[:, pair_lanes[pp]]
            if mode == "fox":
                kx = kx_ref[pp].astype(F32)
            for h in range(2):
                hs = pp * 2 + h
                k_aug = jnp.where(own_half[h], k2, kx).astype(BF16)
                sc = lax.dot_general(qm_sc[hs], k_aug, (((1,), (1,)), ((), ())), preferred_element_type=F32)
                if mode == "dil":
                    sc = sc + lb_ref[delta]
                if causal:
                    sc = jnp.where(keep, sc, NEG)
                off = 0.0 if mode == "fox" else w_tab[(hp0 + pp) * 2 + h] * (ki * t).astype(F32)
                m_old = m_sc[hs]
                m_new = jnp.maximum(m_old, jnp.max(sc, axis=-1, keepdims=True) + off)
                alpha = jnp.exp2(m_old - m_new)
                p = jnp.exp2(sc - _lane_tile(m_new - off, t // LANES))
                v_aug = jnp.where(own_half[h], v2, 1.0).astype(BF16)
                acc_sc[hs] = alpha * acc_sc[hs] + jnp.dot(p.astype(BF16), v_aug, preferred_element_type=F32)
                m_sc[hs] = m_new

    if mode == "dil":
        step(False)
    else:
        pl.when(delta == 0)(lambda: step(True))
        pl.when(delta != 0)(lambda: step(False))

    @pl.when(ki == qi)
    def _():
        for pp in range(pairs):
            a0 = acc_sc[pp * 2]
            a1 = acc_sc[pp * 2 + 1]
            o_ref[:, pair_lanes[pp]] = jnp.where(lo, a0 / a0[:, HEAD_DIM:HEAD_DIM + 1],
                                                 a1 / a1[:, 0:1]).astype(o_ref.dtype)


def _log2_multiplicity(d):
    m = np.zeros(d.shape, np.float64)
    for window, dil in DILATED_BRANCHES:
        m += ((d >= 0) & (d <= window) & (d % dil == 0))
    return np.where(m > 0, np.log2(np.maximum(m, 1.0)), NEG).astype(np.float32)


def _alibi_extras(n_heads, t):
    qx = np.zeros((n_heads, LANES), np.float32)
    w_eff = np.zeros((n_heads,), np.float32)
    for hg, slope in enumerate(_alibi_slopes(n_heads)):
        base = (1 - hg % 2) * HEAD_DIM
        for k, piece in enumerate(_bf16_pieces_np(slope * LOG2E)):
            qx[hg, base + X_POS_HI + k] = POS_SPLIT * piece
            qx[hg, base + X_POS_LO + k] = piece
            w_eff[hg] += piece
    col = np.arange(t)
    kx = np.zeros((t, LANES), np.float32)
    for base in (0, HEAD_DIM):
        for k in range(N_PIECES):
            kx[:, base + X_POS_HI + k] = col // POS_SPLIT
            kx[:, base + X_POS_LO + k] = col % POS_SPLIT
    return qx.reshape(n_heads // 2, 2, LANES), kx, w_eff


def prompt_attention(p, mode, batch, seq_len, n_heads, q_blk, k_blk, v_blk, extra=()):
    t = ATT_TILE
    nq = seq_len // t
    n_hp = n_heads // 2
    pairs = ATT_HEAD_PAIRS
    ng = n_hp // pairs
    wide = pairs * LANES
    win = (WIN_MAX // t) if mode == "dil" else nq
    steps = [(qi, ki) for qi in range(nq) for ki in range(max(0, qi - win), qi + 1)]
    qi_tab = jnp.asarray([s[0] for s in steps], jnp.int32)
    ki_tab = jnp.asarray([s[1] for s in steps], jnp.int32)
    assert t // POS_SPLIT <= 256 and X_BLOCK + seq_len // MOBA_BLOCK <= HEAD_DIM
    assert q_blk % pairs == 0 and k_blk % pairs == 0 and v_blk % pairs == 0 and n_hp % pairs == 0

    def qmap(g, s, qt, kt, wt):
        return ((g // ng) * nq + qt[s], q_blk // pairs + g % ng)

    def kmap(g, s, qt, kt, wt):
        return ((g // ng) * nq + kt[s], k_blk // pairs + g % ng)

    def vmap_(g, s, qt, kt, wt):
        return ((g // ng) * nq + kt[s], v_blk // pairs + g % ng)

    def omap(g, s, qt, kt, wt):
        return ((g // ng) * nq + qt[s], g % ng)

    if mode == "fox":
        kx, = extra
        qx = np.zeros((n_heads, LANES), np.float32)
        for hg in range(n_heads):
            qx[hg, (1 - hg % 2) * HEAD_DIM + X_FORGET:(1 - hg % 2) * HEAD_DIM + X_FORGET + N_PIECES] = 1.0
        qx = qx.reshape(n_hp, 2, LANES)
        w_eff = np.zeros((n_heads,), np.float32)
        kx_spec = pl.BlockSpec((pairs, t, LANES), lambda g, s, qt, kt, wt: (g % ng, (g // ng) * nq + kt[s], 0))
    else:
        qx, kx, w_eff = _alibi_extras(n_heads, t)
        kx = jnp.asarray(kx)
        kx_spec = pl.BlockSpec((t, LANES), lambda g, s, qt, kt, wt: (0, 0))
    in_specs = [pl.BlockSpec((t, wide), qmap), pl.BlockSpec((t, wide), kmap), pl.BlockSpec((t, wide), vmap_),
                pl.BlockSpec((pairs, 2, LANES), lambda g, s, qt, kt, wt: (g % ng, 0, 0)), kx_spec]
    args = [p, p, p, jnp.asarray(qx), kx]
    if mode == "moba":
        km, = extra
        in_specs.append(pl.BlockSpec((1, km.shape[1], wide), lambda g, s, qt, kt, wt: (g // ng, 0, g % ng)))
        args.append(km)
    elif mode == "dil":
        d = (np.arange(win + 1)[:, None, None] * t + np.arange(t)[None, :, None] - np.arange(t)[None, None, :])
        lb = jnp.asarray(_log2_multiplicity(d))
        in_specs.append(pl.BlockSpec(lb.shape, lambda g, s, qt, kt, wt: (0, 0, 0)))
        args.append(lb)
    return pl.pallas_call(
        functools.partial(_flash_body, mode=mode, n_hp=n_hp, win=win),
        out_shape=jax.ShapeDtypeStruct((batch * seq_len, n_heads * HEAD_DIM), BF16),
        grid_spec=pltpu.PrefetchScalarGridSpec(
            num_scalar_prefetch=3,
            grid=(batch * ng, len(steps)),
            in_specs=in_specs,
            out_specs=pl.BlockSpec((t, wide), omap),
            scratch_shapes=[pltpu.VMEM((2 * pairs, t, LANES), BF16), pltpu.VMEM((2 * pairs, t, LANES), F32),
                            pltpu.VMEM((2 * pairs, t, LANES), F32)]),
        compiler_params=_cparams("parallel", "arbitrary"),
        name="prompt_attention_" + mode,
    )(qi_tab, ki_tab, jnp.asarray(w_eff), *args)


def _sample_block_means_body(pt_ref, *refs, group, pages_per_block):
    k_refs, o_ref = refs[:group], refs[group]
    s = pl.program_id(1)

    @pl.when(s == 0)
    def _():
        o_ref[...] = jnp.zeros_like(o_ref)

    lane = lax.broadcasted_iota(jnp.int32, o_ref.shape[1:], 2)
    for g in range(group):
        blk = (s * group + g) // pages_per_block
        part = jnp.sum(k_refs[g][0], axis=-1, keepdims=True) * (1.0 / MOBA_BLOCK)
        o_ref[0] = o_ref[0] + jnp.where(lane == blk, part, 0.0)


def sample_block_means(pool_t, page_table):
    db, n_pages = page_table.shape
    _, n_heads, _, page = pool_t.shape
    group = SAMPLE_PAGE_GROUP
    return pl.pallas_call(
        functools.partial(_sample_block_means_body, group=group, pages_per_block=MOBA_BLOCK // page),
        out_shape=jax.ShapeDtypeStruct((db, n_heads, HEAD_DIM, LANES), F32),
        grid_spec=pltpu.PrefetchScalarGridSpec(
            num_scalar_prefetch=1,
            grid=(db, n_pages // group),
            in_specs=[pl.BlockSpec((1, n_heads, HEAD_DIM, page),
                                   functools.partial(lambda b, s, pt, g: (pt[b * n_pages + s * group + g], 0, 0, 0), g=g))
                      for g in range(group)],
            out_specs=pl.BlockSpec((1, n_heads, HEAD_DIM, LANES), lambda b, s, pt: (b, 0, 0, 0))),
        compiler_params=_cparams("parallel", "arbitrary"),
        name="sample_block_means",
    )(page_table.reshape(-1), *([pool_t] * group))


def _sample_body(pt_ref, q_ref, kn_ref, vn_ref, *rest, mode, n_heads, n_steps, group, page, pages_per_block, paged):
    n_pool = group if paged else 1
    kp, vp, rest = rest[:n_pool], rest[n_pool:2 * n_pool], rest[2 * n_pool:]
    if mode == "moba":
        bias_ref, biasn_ref, kms_ref, o_ref, qbd_sc, kn_sc, vn_sc, acc_sc, m_sc, l_sc, sel_sc = rest
    elif mode == "fox":
        lfp, rest = rest[:group], rest[group:]
        lfn_ref, o_ref, qbd_sc, kn_sc, vn_sc, acc_sc, m_sc, l_sc, ncb_sc, run_sc = rest
    else:
        bias_ref, biasn_ref, mult_ref, multn_ref, o_ref, qbd_sc, kn_sc, vn_sc, acc_sc, m_sc, l_sc = rest
    s = pl.program_id(1)
    nq = q_ref.shape[0]
    rows, width = acc_sc.shape
    lane = lax.broadcasted_iota(jnp.int32, (rows, LANES), 1)
    row_q = lax.broadcasted_iota(jnp.int32, (rows, LANES), 0) % nq
    col_head = lax.broadcasted_iota(jnp.int32, (nq, width), 1) // HEAD_DIM

    def attend(scores, pv, bias, ok, mult):
        logit = scores + bias
        if ok is not None:
            logit = jnp.where(ok, logit, NEG)
        m_old = m_sc[...]
        m_new = jnp.maximum(m_old, jnp.max(logit, axis=-1, keepdims=True))
        alpha = jnp.exp(m_old - m_new)
        p = jnp.exp(logit - _lane_tile(m_new, logit.shape[1] // LANES))
        if mult is not None:
            p = p * mult
        l_sc[...] = alpha * l_sc[...] + jnp.sum(p, axis=-1, keepdims=True)
        acc_sc[...] = _lane_tile(alpha, width // LANES) * acc_sc[...] + pv(p.astype(BF16))
        m_sc[...] = m_new

    @pl.when(s == 0)
    def _():
        acc_sc[...] = jnp.zeros_like(acc_sc)
        l_sc[...] = jnp.zeros_like(l_sc)
        m_sc[...] = jnp.full_like(m_sc, NEG)
        kn_sc[...] = jnp.zeros_like(kn_sc)
        vn_sc[...] = jnp.zeros_like(vn_sc)
        kn_sc[0:nq, :] = kn_ref[...].astype(BF16)
        vn_sc[0:nq, :] = vn_ref[...].astype(BF16)
        q = q_ref[...]
        for h in range(n_heads):
            qbd_sc[h * nq:(h + 1) * nq, :] = jnp.where(col_head == h, q * SCALE, 0.0).astype(BF16)
        causal = lane <= row_q
        scores = lax.dot_general(qbd_sc[...], kn_sc[...], (((1,), (1,)), ((), ())), preferred_element_type=F32)
        pv_new = lambda p: jnp.dot(p, vn_sc[...], preferred_element_type=F32)
        if mode == "moba":
            gate = jnp.concatenate(
                [jnp.dot(q[:, h * HEAD_DIM:(h + 1) * HEAD_DIM], kms_ref[0, h],
                         preferred_element_type=F32, precision=HIGHEST) for h in range(n_heads)], axis=0)
            nb = n_steps * group // pages_per_block
            sel_sc[...] = _top_blocks(gate, float(nb), float(nb), LANES)
            attend(scores, pv_new, biasn_ref[...], causal, None)
        elif mode == "fox":
            lfn = lfn_ref[...]
            tri = (lax.broadcasted_iota(jnp.int32, (nq, nq), 1)
                   <= lax.broadcasted_iota(jnp.int32, (nq, nq), 0)).astype(F32)
            newcum = jnp.dot(tri, lfn, preferred_element_type=F32, precision=HIGHEST)
            eye = lane[:nq] == row_q[:nq]
            ncb, nrow = [], []
            for h in range(n_heads):
                col = jnp.broadcast_to(newcum[:, h:h + 1], (nq, LANES))
                ncb.append(col)
                nrow.append(jnp.broadcast_to(jnp.sum(jnp.where(eye, col, 0.0), axis=0, keepdims=True), (nq, LANES)))
            ncb = jnp.concatenate(ncb, axis=0)
            ncb_sc[...] = ncb
            run_sc[...] = jnp.zeros_like(run_sc)
            attend(scores, pv_new, ncb - jnp.concatenate(nrow, axis=0), causal, None)
        else:
            multn = multn_ref[...]
            attend(scores, pv_new, biasn_ref[...], multn > 0.0, multn)

    if paged:
        k_t = jnp.concatenate([kp[g][0].reshape(width, page) for g in range(group)], axis=1).astype(BF16)
        v_t = jnp.concatenate([vp[g][0].reshape(width, page) for g in range(group)], axis=1).astype(BF16)
    else:
        k_t = kp[0][0].reshape(width, group * page).astype(BF16)
        v_t = vp[0][0].reshape(width, group * page).astype(BF16)
    scores = jnp.dot(qbd_sc[...], k_t, preferred_element_type=F32)
    pv = lambda p: lax.dot_general(p, v_t, (((1,), (1,)), ((), ())), preferred_element_type=F32)
    if mode == "moba":
        hidden = []
        for g in range(group):
            blk = (s * group + g) // pages_per_block
            vis = jnp.max(jnp.where(lane == blk, sel_sc[...], 0.0), axis=-1, keepdims=True)
            hidden.append(jnp.broadcast_to(jnp.where(vis > 0.5, 0.0, NEG), (rows, page)))
        attend(scores, pv, bias_ref[0] + jnp.concatenate(hidden, axis=1), None, None)
    elif mode == "fox":
        lft = jnp.concatenate([lfp[g][0] for g in range(group)], axis=0)
        upper = (lax.broadcasted_iota(jnp.int32, (page, page), 0)
                 > lax.broadcasted_iota(jnp.int32, (page, page), 1)).astype(F32)
        inside = jnp.dot(lft, upper, preferred_element_type=F32, precision=HIGHEST)
        totals = jnp.sum(lft, axis=-1, keepdims=True)
        run = run_sc[...]
        pieces = []
        for g in range(group):
            after = inside[g * n_heads:(g + 1) * n_heads] + run
            pieces.append(jnp.concatenate(
                [jnp.broadcast_to(after[h:h + 1, :], (nq, page)) for h in range(n_heads)], axis=0))
            run = run + totals[g * n_heads:(g + 1) * n_heads]
        run_sc[...] = run
        attend(scores, pv, _lane_tile(ncb_sc[...], group) + jnp.concatenate(pieces, axis=1), None, None)
    else:
        mult = mult_ref[0]
        attend(scores, pv, bias_ref[0], mult > 0.0, mult)

    @pl.when(s == n_steps - 1)
    def _():
        out = acc_sc[...] / _lane_tile(l_sc[...], width // LANES)
        o_ref[...] = sum(jnp.where(col_head == h, out[h * nq:(h + 1) * nq, :], 0.0) for h in range(n_heads))


def _multiplicity(d):
    m = np.zeros(d.shape, np.float32)
    for window, dil in DILATED_BRANCHES:
        m += ((d >= 0) & (d <= window) & (d % dil == 0)).astype(np.float32)
    return m


def _sample_tables(past_len, nq, n_heads, n_steps, span):
    d = (past_len + np.arange(nq)[None, :, None]) - (np.arange(n_steps)[:, None, None] * span
                                                     + np.arange(span)[None, None, :])
    dn = np.arange(nq)[:, None] - np.arange(LANES)[None, :]
    slopes = np.repeat(np.asarray(_alibi_slopes(n_heads), np.float64), nq)[None, :, None]
    tile = lambda a: np.tile(a, (1, n_heads, 1))
    d, dn = tile(d), tile(dn[None])
    return ((-slopes * d).astype(np.float32), (-slopes * dn)[0].astype(np.float32),
            _multiplicity(d), _multiplicity(dn)[0])


def sample_attention(p, mode, row0, db, nq, n_heads, q_blk, k_blk, v_blk, k_pool_t, v_pool_t, page_table, extra=()):
    width = n_heads * HEAD_DIM
    rows = n_heads * nq
    group = SAMPLE_PAGE_GROUP
    reverse = mode == "fox"
    paged = page_table is not None
    page = k_pool_t.shape[3] if paged else LANES
    assert page == LANES
    n_pages = page_table.shape[1] if paged else k_pool_t.shape[3] // page
    pt = page_table.reshape(-1) if paged else jnp.zeros((1,), jnp.int32)
    n_steps = n_pages // group
    span = group * page
    rb0 = row0 // nq

    def pg(s, g):
        j = s * group + g
        return (n_pages - 1 - j) if reverse else j

    def pool_spec(g):
        return pl.BlockSpec((1, n_heads, HEAD_DIM, page), lambda b, s, t: (t[b * n_pages + pg(s, g)], 0, 0, 0))

    if paged:
        pool_specs = [pool_spec(g) for g in range(group)]
        pools = lambda a: [a] * group
    else:
        pool_specs = [pl.BlockSpec((1, n_heads, HEAD_DIM, span), lambda b, s, t: (b, 0, 0, s))]
        pools = lambda a: [a]
    bias, biasn, mult, multn = [jnp.asarray(a) for a in _sample_tables(n_pages * page, nq, n_heads, n_steps, span)]
    table_spec = pl.BlockSpec((1, rows, span), lambda b, s, t: (s, 0, 0))
    new_spec = pl.BlockSpec((rows, LANES), lambda b, s, t: (0, 0))
    in_specs = [pl.BlockSpec((nq, width), lambda b, s, t: (rb0 + b, q_blk)),
                pl.BlockSpec((nq, width), lambda b, s, t: (rb0 + b, k_blk)),
                pl.BlockSpec((nq, width), lambda b, s, t: (rb0 + b, v_blk))]
    in_specs += pool_specs * 2
    args = [p, p, p] + pools(k_pool_t) + pools(v_pool_t)
    scratch = [pltpu.VMEM((rows, width), BF16), pltpu.VMEM((LANES, width), BF16), pltpu.VMEM((LANES, width), BF16),
               pltpu.VMEM((rows, width), F32), pltpu.VMEM((rows, LANES), F32), pltpu.VMEM((rows, LANES), F32)]
    if mode == "moba":
        kms, = extra
        in_specs += [table_spec, new_spec, pl.BlockSpec((1,) + kms.shape[1:], lambda b, s, t: (b, 0, 0, 0))]
        args += [bias, biasn, kms]
        scratch.append(pltpu.VMEM((rows, LANES), F32))
    elif mode == "fox":
        lf_pool_t, logf = extra
        in_specs += [pl.BlockSpec((1, n_heads, page),
                                  functools.partial(lambda b, s, t, g: (t[b * n_pages + pg(s, g)], 0, 0), g=g))
                     for g in range(group)]
        in_specs.append(pl.BlockSpec((nq, LANES), lambda b, s, t: (rb0 + b, 0)))
        args += [lf_pool_t] * group + [logf]
        scratch += [pltpu.VMEM((rows, LANES), F32), pltpu.VMEM((n_heads, LANES), F32)]
    else:
        in_specs += [table_spec, new_spec, table_spec, new_spec]
        args += [bias, biasn, mult, multn]
    return pl.pallas_call(
        functools.partial(_sample_body, mode=mode, n_heads=n_heads, n_steps=n_steps, group=group, page=page,
                          pages_per_block=MOBA_BLOCK // page, paged=paged),
        out_shape=jax.ShapeDtypeStruct((db * nq, width), F32),
        grid_spec=pltpu.PrefetchScalarGridSpec(
            num_scalar_prefetch=1,
            grid=(db, n_steps),
            in_specs=in_specs,
            out_specs=pl.BlockSpec((nq, width), lambda b, s, t: (b, 0)),
            scratch_shapes=scratch),
        compiler_params=_cparams("parallel", "arbitrary"),
        name="sample_attention_" + mode,
    )(pt, *args)


RANK_NONE = 127.0


def _top_rows(x, k, with_rank=False):
    vals = []
    rank = jnp.full(x.shape, RANK_NONE, F32)
    for r in range(k):
        m = jnp.max(x, axis=0, keepdims=True)
        vals.append(m)
        hit = x == m
        if with_rank:
            rank = jnp.where(hit, float(r), rank)
        x = jnp.where(hit, -jnp.inf, x)
    return (vals, rank) if with_rank else vals


def _bf16_twice(x):
    bits = lax.bitcast_convert_type(x.astype(BF16).astype(F32), jnp.uint32)
    return bits | (bits >> 16)


def _peer_scores_body(ht_ref, wq_ref, sk_ref, n1_ref, r2_ref, e1_ref, e2_ref):
    half = PEER_KEYS
    k1 = PEER_TOPK + 1
    qt = jnp.dot(wq_ref[...], ht_ref[...], preferred_element_type=F32)
    s1 = jnp.dot(sk_ref[0, 0], qt[:half], preferred_element_type=F32, precision=HIGHEST)
    s2 = jnp.dot(sk_ref[0, 1], qt[half:], preferred_element_type=F32, precision=HIGHEST)
    t1 = _top_rows(s1, k1)
    t2, rank2 = _top_rows(s2, k1, with_rank=True)
    cand = [t1[i] + t2[j] for i in range(k1) for j in range(k1) if (i + 1) * (j + 1) <= k1]
    pad = -len(cand) % SUBLANES
    cand = jnp.concatenate(cand + [jnp.full_like(t1[0], -jnp.inf)] * pad, axis=0)
    top = _top_rows(cand, k1)
    z = sum(jnp.exp(v - top[0]) for v in top[:PEER_TOPK])
    tau = 0.5 * (top[PEER_TOPK - 1] + top[PEER_TOPK])
    need = tau - s1
    n1 = jnp.zeros_like(s1)
    for j in range(k1):
        n1 = jnp.where(t2[j] >= need, float(j + 1), n1)
    n1_ref[0] = _bf16_twice(n1)
    r2_ref[0] = pltpu.bitcast(rank2.astype(BF16), jnp.uint32)
    e1_ref[0] = _bf16_twice(jnp.exp(s1 - t1[0]) / z)
    e2_ref[0] = pltpu.bitcast(jnp.exp(s2 - t2[0]).astype(BF16), jnp.uint32)


def peer_scores(ht, wq_t, subkeys):
    d, t = ht.shape
    tt = PEER_TOKEN_TILE
    nh = PEER_HEADS
    dq = wq_t.shape[0] // nh
    big = jax.ShapeDtypeStruct((nh, PEER_KEYS, t), jnp.uint32)
    small = jax.ShapeDtypeStruct((nh, PEER_KEYS // 2, t), jnp.uint32)
    bspec = pl.BlockSpec((1, PEER_KEYS, tt), lambda i, h: (h, 0, i))
    pspec = pl.BlockSpec((1, PEER_KEYS // 2, tt), lambda i, h: (h, 0, i))
    return pl.pallas_call(
        _peer_scores_body,
        out_shape=(big, small, big, small),
        grid=(t // tt, nh),
        in_specs=[pl.BlockSpec((d, tt), lambda i, h: (0, i)),
                  pl.BlockSpec((dq, d), lambda i, h: (h, 0)),
                  pl.BlockSpec((1,) + subkeys.shape[1:], lambda i, h: (h, 0, 0, 0))],
        out_specs=(bspec, pspec, bspec, pspec),
        compiler_params=_cparams("parallel", "arbitrary"),
        name="peer_scores",
    )(ht, wq_t, subkeys)


def _gelu(x):
    return 0.5 * x * (1.0 + lax.erf(x * (1.0 / math.sqrt(2.0))))


def _peer_experts_body(ht_ref, u_ref, vt_ref, n1_ref, r2_ref, e1_ref, e2_ref, x_ref, g_ref,
                       xn_ref, hn_ref, acc_sc, act_sc, w_sc):
    j = pl.program_id(1)
    te = u_ref.shape[0]
    tt = ht_ref.shape[1]
    groups = te // PEER_KEYS

    @pl.when(j == 0)
    def _():
        acc_sc[...] = jnp.zeros_like(acc_sc)

    act_sc[...] = _gelu(jnp.dot(u_ref[...], ht_ref[...], preferred_element_type=F32)).astype(BF16)
    n_tc = tt // LANES
    packed_rows = (PEER_KEYS // 2, LANES)
    for al in range(groups):
        rows = slice(al * PEER_KEYS, (al + 1) * PEER_KEYS)
        coef = [jnp.zeros((PEER_KEYS, LANES), BF16) for _ in range(n_tc)]
        for h in range(PEER_HEADS):
            n1_row = n1_ref[h, al:al + 1, :]
            e1_row = e1_ref[h, al:al + 1, :]
            for tc in range(n_tc):
                cols = slice(tc * LANES, (tc + 1) * LANES)
                n1 = pltpu.bitcast(jnp.broadcast_to(n1_row[:, cols], packed_rows), BF16)
                e1 = pltpu.bitcast(jnp.broadcast_to(e1_row[:, cols], packed_rows), BF16)
                hit = pltpu.bitcast(r2_ref[h, :, cols], BF16) < n1
                e2 = pltpu.bitcast(e2_ref[h, :, cols], BF16)
                coef[tc] = coef[tc] + jnp.where(hit, e2, jnp.zeros((), BF16)) * e1
        for tc in range(n_tc):
            cols = slice(tc * LANES, (tc + 1) * LANES)
            w_sc[rows, cols] = coef[tc] * act_sc[rows, cols]
    acc_sc[...] += jnp.dot(vt_ref[...], w_sc[...], preferred_element_type=F32)

    @pl.when(j == pl.num_programs(1) - 1)
    def _():
        xn = x_ref[...] + acc_sc[...].T
        xn_ref[...] = xn
        hn_ref[...] = _rms(xn, g_ref[...]).astype(hn_ref.dtype)


def peer_experts(x, ht, u, v_t, scores, g_next, norm_dtype):
    t, d = x.shape
    tt = PEER_TOKEN_TILE
    te = PEER_EXPERT_TILE
    n_exp = u.shape[0]
    sspec = pl.BlockSpec((PEER_HEADS, te // PEER_KEYS, tt), lambda i, j: (0, j, i))
    pspec = pl.BlockSpec((PEER_HEADS, PEER_KEYS // 2, tt), lambda i, j: (0, 0, i))
    return pl.pallas_call(
        _peer_experts_body,
        out_shape=(jax.ShapeDtypeStruct((t, d), F32), jax.ShapeDtypeStruct((t, d), norm_dtype)),
        grid=(t // tt, n_exp // te),
        in_specs=[pl.BlockSpec((d, tt), lambda i, j: (0, i)),
                  pl.BlockSpec((te, d), lambda i, j: (j, 0)),
                  pl.BlockSpec((d, te), lambda i, j: (0, j)),
                  sspec, pspec, sspec, pspec,
                  pl.BlockSpec((tt, d), lambda i, j: (i, 0)),
                  pl.BlockSpec((1, d), lambda i, j: (0, 0))],
        out_specs=(pl.BlockSpec((tt, d), lambda i, j: (i, 0)),
                   pl.BlockSpec((tt, d), lambda i, j: (i, 0))),
        scratch_shapes=[pltpu.VMEM((d, tt), F32), pltpu.VMEM((te, tt), BF16), pltpu.VMEM((te, tt), BF16)],
        compiler_params=_cparams("parallel", "arbitrary"),
        name="peer_experts",
    )(ht, u, v_t, *scores, x, g_next.reshape(1, d))


def _heads(a, n):
    return a.reshape(a.shape[:-1] + (n, HEAD_DIM))


def kernel(x_prompt, x_sample, cache_moba_k, cache_moba_v, cache_fox_k, cache_fox_v, cache_fox_logf,
           state_swa_k, state_swa_v, page_table, norm_gain, final_gain, w_in_even, b_forget, w_out_even,
           w_in_odd, w_out_odd, peer_w_query, peer_subkeys, peer_u, peer_v):
    b, s, d = x_prompt.shape
    db, ds, _ = x_sample.shape
    tp = b * s
    depth = norm_gain.shape[0]
    n_pool, page = cache_moba_k.shape[1:3]
    past_len = page_table.shape[1] * page
    win_buf = state_swa_k.shape[2]
    keep_p = min(WIN_MAX, s)
    assert win_buf == past_len and past_len % MOBA_BLOCK == 0 and MOBA_BLOCK % page == 0
    assert s % ATT_TILE == 0 and (tp + db * ds) % ROW_TILE == 0 and tp % ds == 0

    x = jnp.concatenate([x_prompt.reshape(tp, d), x_sample.reshape(db * ds, d)], axis=0)
    h = rmsnorm_rows(x, norm_gain[0, 0], BF16)
    new = {}
    for layer in range(depth):
        li = layer // 2
        if layer % 2 == 0:
            n_in = w_in_even.shape[2]
            n_pad = -n_in % (5 * LANES)
            w_in = jnp.pad(w_in_even[li], ((0, 0), (0, n_pad))).astype(BF16)
            p = matmul(h, w_in, (n_in + n_pad) // 5)
            logf, fox_kx = fox_gate(p, 3 * (D_A + D_B) // LANES, b_forget[li], s)
            km = moba_block_means(p, 1, b, s)
            oa_p = prompt_attention(p, "moba", b, s, N_HEADS_A, 0, D_A // LANES, 2 * D_A // LANES, (km,))
            fb = 3 * D_A // LANES
            ob_p = prompt_attention(p, "fox", b, s, N_HEADS_B, fb, fb + D_B // LANES, fb + 2 * D_B // LANES,
                                    (fox_kx,))
            pool = lambda c: jnp.transpose(c[li], (0, 2, 3, 1))
            ka_pool, va_pool = pool(cache_moba_k), pool(cache_moba_v)
            kb_pool, vb_pool = pool(cache_fox_k), pool(cache_fox_v)
            kms = sample_block_means(ka_pool, page_table)
            oa_s = sample_attention(p, "moba", tp, db, ds, N_HEADS_A, 0, 1, 2, ka_pool, va_pool, page_table, (kms,))
            lf_pool_t = jnp.swapaxes(cache_fox_logf[li], 1, 2)
            ob_s = sample_attention(p, "fox", tp, db, ds, N_HEADS_B, 3, 4, 5, kb_pool, vb_pool, page_table,
                                    (lf_pool_t, logf))
            w_out = w_out_even[li].astype(BF16)
            pairs = [(jnp.concatenate([oa_p, oa_s.astype(BF16)], axis=0), w_out[:D_A]),
                     (jnp.concatenate([ob_p, ob_s.astype(BF16)], axis=0), w_out[D_A:])]
            cuts = {"mk": (D_A, N_HEADS_A), "mv": (2 * D_A, N_HEADS_A), "fk": (3 * D_A + D_B, N_HEADS_B),
                    "fv": (3 * D_A + 2 * D_B, N_HEADS_B)}
            for name, (c0, nh) in cuts.items():
                cols = p[:, c0:c0 + nh * HEAD_DIM]
                new.setdefault("p_" + name, []).append(_heads(cols[:tp].reshape(b, s, -1), nh))
                new.setdefault("s_" + name, []).append(_heads(cols[tp:].reshape(db, ds, -1), nh))
            new.setdefault("p_fl", []).append(logf[:tp, :N_HEADS_B].reshape(b, s, N_HEADS_B))
            new.setdefault("s_fl", []).append(logf[tp:, :N_HEADS_B].reshape(db, ds, N_HEADS_B))
        else:
            p = matmul(h, w_in_odd[li].astype(BF16), D_C // 2)
            nb = D_C // LANES
            oc_p = prompt_attention(p, "dil", b, s, N_HEADS_C, 0, nb, 2 * nb)
            kc_pool = jnp.transpose(state_swa_k[li], (0, 2, 3, 1))
            vc_pool = jnp.transpose(state_swa_v[li], (0, 2, 3, 1))
            oc_s = sample_attention(p, "dil", tp, db, ds, N_HEADS_C, 0, 1, 2, kc_pool, vc_pool, None)
            pairs = [(jnp.concatenate([oc_p, oc_s.astype(BF16)], axis=0), w_out_odd[li].astype(BF16))]
            for name, c0 in (("sk", D_C), ("sv", 2 * D_C)):
                cols = p[:, c0:c0 + D_C]
                new.setdefault("p_" + name, []).append(
                    _heads(cols[:tp].reshape(b, s, -1)[:, s - keep_p:], N_HEADS_C))
                new.setdefault("s_" + name, []).append(_heads(cols[tp:].reshape(db, ds, -1), N_HEADS_C))
        x, _, h_t = resid_norm(x, pairs, norm_gain[layer, 1])
        scores = peer_scores(h_t, peer_w_query[layer].T.astype(BF16), peer_subkeys[layer])
        last = layer == depth - 1
        g_next = final_gain if last else norm_gain[layer + 1, 0]
        x, h = peer_experts(x, h_t, peer_u[layer].astype(BF16), peer_v[layer].T.astype(BF16), scores,
                            g_next, F32 if last else BF16)
    y = h
    order = ["p_mk", "p_mv", "p_fk", "p_fv", "p_fl", "p_sk", "p_sv",
             "s_mk", "s_mv", "s_fk", "s_fv", "s_fl", "s_sk", "s_sv"]
    return (y[:tp].reshape(b, s, d), y[tp:].reshape(db, ds, d)) + tuple(jnp.stack(new[k]) for k in order)
```

```python
import functools
import math

import numpy as np
import jax
import jax.numpy as jnp
from jax import lax
from jax.experimental import pallas as pl
from jax.experimental.pallas import tpu as pltpu

F32 = jnp.float32
BF16 = jnp.bfloat16
HIGHEST = lax.Precision.HIGHEST

LANES = 128
SUBLANES = 8
VMEM_LIMIT_BYTES = 56 * 1024 * 1024

HEAD_DIM = 64
N_HEADS_A = 8
N_HEADS_B = 8
N_HEADS_C = 16
D_A = N_HEADS_A * HEAD_DIM
D_B = N_HEADS_B * HEAD_DIM
D_C = N_HEADS_C * HEAD_DIM
MOBA_BLOCK = 256
MOBA_TOPK = 3
DILATED_BRANCHES = ((128, 1), (512, 4), (2048, 16))
WIN_MAX = 2048
PEER_HEADS = 8
PEER_KEYS = 128
PEER_TOPK = 16
RMS_EPS = 1e-6
NEG = -1e30
SCALE = HEAD_DIM ** -0.5
LOG2E = 1.0 / math.log(2.0)

ATT_TILE = 512
ATT_HEAD_PAIRS = 2
GATE_TILE = 256
ROW_TILE = 512
PEER_TOKEN_TILE = 512
PEER_EXPERT_TILE = 1024
SAMPLE_PAGE_GROUP = 8

X_POS_HI = 0
X_POS_LO = 3
X_FORGET = 0
X_BLOCK = 8
POS_SPLIT = 16
N_PIECES = 3


def _cparams(*sem):
    return pltpu.CompilerParams(dimension_semantics=sem, vmem_limit_bytes=VMEM_LIMIT_BYTES)


def _alibi_slopes(n):
    return [2.0 ** (-8.0 * (i + 1) / n) for i in range(n)]


def _bf16_pieces_np(x):
    x = np.asarray(x, np.float32)
    out = []
    for _ in range(N_PIECES):
        p = np.asarray(np.asarray(x, dtype=BF16), np.float32)
        out.append(p)
        x = x - p
    return out


def _rms(x, g):
    return x * lax.rsqrt(jnp.mean(x * x, axis=-1, keepdims=True) + RMS_EPS) * g


def _lane_tile(x, n):
    return x if n == 1 else jnp.concatenate([x] * n, axis=1)


def _rms_body(x_ref, g_ref, h_ref):
    h_ref[...] = _rms(x_ref[...], g_ref[...]).astype(h_ref.dtype)


def rmsnorm_rows(x, g, out_dtype):
    t, d = x.shape
    return pl.pallas_call(
        _rms_body,
        out_shape=jax.ShapeDtypeStruct((t, d), out_dtype),
        grid=(t // ROW_TILE,),
        in_specs=[pl.BlockSpec((ROW_TILE, d), lambda i: (i, 0)),
                  pl.BlockSpec((1, d), lambda i: (0, 0))],
        out_specs=pl.BlockSpec((ROW_TILE, d), lambda i: (i, 0)),
        compiler_params=_cparams("parallel"),
        name="rmsnorm_rows",
    )(x, g.reshape(1, d))


def _mm_body(h_ref, w_ref, o_ref):
    o_ref[...] = jnp.dot(h_ref[...], w_ref[...], preferred_element_type=F32)


def matmul(h, w, tn):
    t, k = h.shape
    n = w.shape[1]
    return pl.pallas_call(
        _mm_body,
        out_shape=jax.ShapeDtypeStruct((t, n), F32),
        grid=(t // ROW_TILE, n // tn),
        in_specs=[pl.BlockSpec((ROW_TILE, k), lambda i, j: (i, 0)),
                  pl.BlockSpec((k, tn), lambda i, j: (0, j))],
        out_specs=pl.BlockSpec((ROW_TILE, tn), lambda i, j: (i, j)),
        compiler_params=_cparams("parallel", "arbitrary"),
        name="matmul",
    )(h, w)


def _resid_norm_body(*refs, n_pairs):
    x_ref = refs[0]
    g_ref = refs[1 + 2 * n_pairs]
    xn_ref, h_ref, ht_ref = refs[2 + 2 * n_pairs:]
    acc = x_ref[...]
    for p in range(n_pairs):
        acc = acc + jnp.dot(refs[1 + 2 * p][...], refs[2 + 2 * p][...], preferred_element_type=F32)
    xn_ref[...] = acc
    h = _rms(acc, g_ref[...])
    h_ref[...] = h.astype(h_ref.dtype)
    ht_ref[...] = h.T.astype(ht_ref.dtype)


def resid_norm(x, pairs, g):
    t, d = x.shape
    args = [x]
    in_specs = [pl.BlockSpec((ROW_TILE, d), lambda i: (i, 0))]
    for a, w in pairs:
        args += [a, w]
        in_specs += [pl.BlockSpec((ROW_TILE, a.shape[1]), lambda i: (i, 0)),
                     pl.BlockSpec(w.shape, lambda i: (0, 0))]
    args.append(g.reshape(1, d))
    in_specs.append(pl.BlockSpec((1, d), lambda i: (0, 0)))
    return pl.pallas_call(
        functools.partial(_resid_norm_body, n_pairs=len(pairs)),
        out_shape=(jax.ShapeDtypeStruct((t, d), F32),
                   jax.ShapeDtypeStruct((t, d), BF16),
                   jax.ShapeDtypeStruct((d, t), BF16)),
        grid=(t // ROW_TILE,),
        in_specs=in_specs,
        out_specs=(pl.BlockSpec((ROW_TILE, d), lambda i: (i, 0)),
                   pl.BlockSpec((ROW_TILE, d), lambda i: (i, 0)),
                   pl.BlockSpec((d, ROW_TILE), lambda i: (0, i))),
        compiler_params=_cparams("parallel"),
        name="resid_norm",
    )(*args)


def _log_sigmoid(x):
    return -(jnp.maximum(-x, 0.0) + jnp.log1p(jnp.exp(-jnp.abs(x))))


def _fox_gate_body(p_ref, b_ref, place_ref, lf_ref, kx_ref, carry_sc, *, tiles_per_seq):
    i = pl.program_id(0)
    t = p_ref.shape[0]

    @pl.when(i % tiles_per_seq == 0)
    def _():
        carry_sc[...] = jnp.zeros_like(carry_sc)

    lf = _log_sigmoid(p_ref[...] + b_ref[...])
    lf_ref[...] = lf
    row = lax.broadcasted_iota(jnp.int32, (t, t), 0)
    col = lax.broadcasted_iota(jnp.int32, (t, t), 1)
    tri = (col <= row).astype(F32)
    c = jnp.dot(tri, lf, preferred_element_type=F32, precision=HIGHEST) + carry_sc[0:1, :]
    carry_sc[...] = jnp.broadcast_to(c[t - 1:t, :], carry_sc.shape)
    rest = c * LOG2E
    pieces = []
    for _ in range(N_PIECES):
        piece = rest.astype(BF16)
        pieces.append(piece)
        rest = rest - piece.astype(F32)
    for hp in range(kx_ref.shape[0]):
        moved = sum(jnp.dot(pieces[k], place_ref[hp, k], preferred_element_type=F32) for k in range(N_PIECES))
        kx_ref[hp] = (-moved).astype(BF16)


def fox_gate(p, col_block, b_forget, seq_len):
    t = p.shape[0]
    n_heads = b_forget.shape[0]
    n_hp = n_heads // 2
    b = jnp.zeros((1, LANES), F32).at[0, :n_heads].set(b_forget)
    place = np.zeros((n_hp, N_PIECES, LANES, LANES), np.float32)
    for hp in range(n_hp):
        for k in range(N_PIECES):
            place[hp, k, 2 * hp, HEAD_DIM + X_FORGET + k] = 1.0
            place[hp, k, 2 * hp + 1, X_FORGET + k] = 1.0
    return pl.pallas_call(
        functools.partial(_fox_gate_body, tiles_per_seq=seq_len // GATE_TILE),
        out_shape=(jax.ShapeDtypeStruct((t, LANES), F32),
                   jax.ShapeDtypeStruct((n_hp, t, LANES), BF16)),
        grid=(t // GATE_TILE,),
        in_specs=[pl.BlockSpec((GATE_TILE, LANES), lambda i: (i, col_block)),
                  pl.BlockSpec((1, LANES), lambda i: (0, 0)),
                  pl.BlockSpec(place.shape, lambda i: (0, 0, 0, 0))],
        out_specs=(pl.BlockSpec((GATE_TILE, LANES), lambda i: (i, 0)),
                   pl.BlockSpec((n_hp, GATE_TILE, LANES), lambda i: (0, i, 0))),
        scratch_shapes=[pltpu.VMEM((SUBLANES, LANES), F32)],
        compiler_params=_cparams("arbitrary"),
        name="fox_gate",
    )(p, b, jnp.asarray(place, BF16))


def _kmean_body(k_ref, o_ref):
    j = pl.program_id(1)
    o_ref[0, pl.ds(j, 1), :] = jnp.sum(k_ref[...], axis=0, keepdims=True) * (1.0 / MOBA_BLOCK)


def moba_block_means(p, k_col_block, batch, seq_len):
    nb = seq_len // MOBA_BLOCK
    return pl.pallas_call(
        _kmean_body,
        out_shape=jax.ShapeDtypeStruct((batch, nb, D_A), F32),
        grid=(batch, nb),
        in_specs=[pl.BlockSpec((MOBA_BLOCK, D_A), lambda b, j: (b * nb + j, k_col_block))],
        out_specs=pl.BlockSpec((1, nb, D_A), lambda b, j: (b, 0, 0)),
        compiler_params=_cparams("parallel", "arbitrary"),
        name="moba_block_means",
    )(p)


def _top_blocks(gate, n_valid, own, n_blocks, axis=1):
    jidx = lax.broadcasted_iota(jnp.int32, gate.shape, axis).astype(F32)
    gate = jnp.where(jidx < n_valid, gate, NEG)
    sel = (jidx == own).astype(F32)
    for _ in range(MOBA_TOPK):
        mx = jnp.max(gate, axis=axis, keepdims=True)
        am = jnp.min(jnp.where(gate == mx, jidx, float(n_blocks)), axis=axis, keepdims=True)
        hit = jidx == am
        sel = jnp.where(jnp.logical_and(hit, am < n_valid), 1.0, sel)
        gate = jnp.where(hit, -jnp.inf, gate)
    return sel


def _flash_body(qi_tab, ki_tab, w_tab, q_ref, k_ref, v_ref, qx_ref, kx_ref, *rest, mode, n_hp, win):
    if mode == "moba":
        km_ref, o_ref, qm_sc, acc_sc, m_sc = rest
    elif mode == "fox":
        o_ref, qm_sc, acc_sc, m_sc = rest
    else:
        lb_ref, o_ref, qm_sc, acc_sc, m_sc = rest
    t = q_ref.shape[0]
    pairs = q_ref.shape[1] // LANES
    hp0 = (pl.program_id(0) % (n_hp // pairs)) * pairs
    s = pl.program_id(1)
    qi = qi_tab[s]
    ki = ki_tab[s]
    delta = qi - ki
    lane = lax.broadcasted_iota(jnp.int32, (t, LANES), 1)
    lo = lane < HEAD_DIM
    own_half = (lo, jnp.logical_not(lo))
    pair_lanes = [slice(pp * LANES, (pp + 1) * LANES) for pp in range(pairs)]

    @pl.when(ki == jnp.maximum(qi - win, 0))
    def _():
        acc_sc[...] = jnp.zeros_like(acc_sc)
        m_sc[...] = jnp.full_like(m_sc, NEG)
        for pp in range(pairs):
            q = q_ref[:, pair_lanes[pp]]
            qs = q * (SCALE * LOG2E)
            for h in range(2):
                qm = jnp.where(own_half[h], qs, qx_ref[pp, h:h + 1, :])
                if mode == "moba":
                    km = km_ref[0, :, pair_lanes[pp]]
                    nb = km.shape[0]
                    gate = lax.dot_general(km, jnp.where(own_half[h], q, 0.0), (((1,), (1,)), ((), ())),
                                           preferred_element_type=F32, precision=HIGHEST)
                    own_blk = (qi * (t // MOBA_BLOCK)
                               + lax.broadcasted_iota(jnp.int32, (1, t), 1) // MOBA_BLOCK).astype(F32)
                    hidden = (_top_blocks(gate, own_blk, own_blk, nb, axis=0) - 1.0) * (-NEG)
                    base = (1 - h) * HEAD_DIM + X_BLOCK
                    place = (lax.broadcasted_iota(jnp.int32, (nb, LANES), 1)
                             == lax.broadcasted_iota(jnp.int32, (nb, LANES), 0) + base).astype(BF16)
                    qm = qm + lax.dot_general(hidden.astype(BF16), place, (((0,), (0,)), ((), ())),
                                              preferred_element_type=F32)
                qm_sc[pp * 2 + h] = qm.astype(BF16)

    def step(causal):
        if mode != "fox":
            kx = kx_ref[...]
        if mode == "moba":
            key_blk = ki * (t // MOBA_BLOCK) + lax.broadcasted_iota(jnp.int32, (t, LANES), 0) // MOBA_BLOCK
            kx = jnp.where((lane % HEAD_DIM) == key_blk + X_BLOCK, 1.0, kx)
        if causal:
            keep = (lax.broadcasted_iota(jnp.int32, (t, t), 1) <= lax.broadcasted_iota(jnp.int32, (t, t), 0))
        for pp in range(pairs):
            k2 = k_ref[:, pair_lanes[pp]]
            v2 = v_ref[:, pair_lanes[pp]]
            if mode == "fox":
                kx = kx_ref[pp].astype(F32)
            for h in range(2):
                hs = pp * 2 + h
                k_aug = jnp.where(own_half[h], k2, kx).astype(BF16)
                sc = lax.dot_general(qm_sc[hs], k_aug, (((1,), (1,)), ((), ())), preferred_element_type=F32)
                if mode == "dil":
                    sc = sc + lb_ref[delta]
                if causal:
                    sc = jnp.where(keep, sc, NEG)
                off = 0.0 if mode == "fox" else w_tab[(hp0 + pp) * 2 + h] * (ki * t).astype(F32)
                m_old = m_sc[hs]
                m_new = jnp.maximum(m_old, jnp.max(sc, axis=-1, keepdims=True) + off)
                alpha = jnp.exp2(m_old - m_new)
                p = jnp.exp2(sc - _lane_tile(m_new - off, t // LANES))
                v_aug = jnp.where(own_half[h], v2, 1.0).astype(BF16)
                acc_sc[hs] = alpha * acc_sc[hs] + jnp.dot(p.astype(BF16), v_aug, preferred_element_type=F32)
                m_sc[hs] = m_new

    if mode == "dil":
        step(False)
    else:
        pl.when(delta == 0)(lambda: step(True))
        pl.when(delta != 0)(lambda: step(False))

    @pl.when(ki == qi)
    def _():
        for pp in range(pairs):
            a0 = acc_sc[pp * 2]
            a1 = acc_sc[pp * 2 + 1]
            o_ref[:, pair_lanes[pp]] = jnp.where(lo, a0 / a0[:, HEAD_DIM:HEAD_DIM + 1],
                                                 a1 / a1[:, 0:1]).astype(o_ref.dtype)


def _log2_multiplicity(d):
    m = np.zeros(d.shape, np.float64)
    for window, dil in DILATED_BRANCHES:
        m += ((d >= 0) & (d <= window) & (d % dil == 0))
    return np.where(m > 0, np.log2(np.maximum(m, 1.0)), NEG).astype(np.float32)


def _alibi_extras(n_heads, t):
    qx = np.zeros((n_heads, LANES), np.float32)
    w_eff = np.zeros((n_heads,), np.float32)
    for hg, slope in enumerate(_alibi_slopes(n_heads)):
        base = (1 - hg % 2) * HEAD_DIM
        for k, piece in enumerate(_bf16_pieces_np(slope * LOG2E)):
            qx[hg, base + X_POS_HI + k] = POS_SPLIT * piece
            qx[hg, base + X_POS_LO + k] = piece
            w_eff[hg] += piece
    col = np.arange(t)
    kx = np.zeros((t, LANES), np.float32)
    for base in (0, HEAD_DIM):
        for k in range(N_PIECES):
            kx[:, base + X_POS_HI + k] = col // POS_SPLIT
            kx[:, base + X_POS_LO + k] = col % POS_SPLIT
    return qx.reshape(n_heads // 2, 2, LANES), kx, w_eff


def prompt_attention(p, mode, batch, seq_len, n_heads, q_blk, k_blk, v_blk, extra=()):
    t = ATT_TILE
    nq = seq_len // t
    n_hp = n_heads // 2
    pairs = ATT_HEAD_PAIRS
    ng = n_hp // pairs
    wide = pairs * LANES
    win = (WIN_MAX // t) if mode == "dil" else nq
    steps = [(qi, ki) for qi in range(nq) for ki in range(max(0, qi - win), qi + 1)]
    qi_tab = jnp.asarray([s[0] for s in steps], jnp.int32)
    ki_tab = jnp.asarray([s[1] for s in steps], jnp.int32)
    assert t // POS_SPLIT <= 256 and X_BLOCK + seq_len // MOBA_BLOCK <= HEAD_DIM
    assert q_blk % pairs == 0 and k_blk % pairs == 0 and v_blk % pairs == 0 and n_hp % pairs == 0

    def qmap(g, s, qt, kt, wt):
        return ((g // ng) * nq + qt[s], q_blk // pairs + g % ng)

    def kmap(g, s, qt, kt, wt):
        return ((g // ng) * nq + kt[s], k_blk // pairs + g % ng)

    def vmap_(g, s, qt, kt, wt):
        return ((g // ng) * nq + kt[s], v_blk // pairs + g % ng)

    def omap(g, s, qt, kt, wt):
        return ((g // ng) * nq + qt[s], g % ng)

    if mode == "fox":
        kx, = extra
        qx = np.zeros((n_heads, LANES), np.float32)
        for hg in range(n_heads):
            qx[hg, (1 - hg % 2) * HEAD_DIM + X_FORGET:(1 - hg % 2) * HEAD_DIM + X_FORGET + N_PIECES] = 1.0
        qx = qx.reshape(n_hp, 2, LANES)
        w_eff = np.zeros((n_heads,), np.float32)
        kx_spec = pl.BlockSpec((pairs, t, LANES), lambda g, s, qt, kt, wt: (g % ng, (g // ng) * nq + kt[s], 0))
    else:
        qx, kx, w_eff = _alibi_extras(n_heads, t)
        kx = jnp.asarray(kx)
        kx_spec = pl.BlockSpec((t, LANES), lambda g, s, qt, kt, wt: (0, 0))
    in_specs = [pl.BlockSpec((t, wide), qmap), pl.BlockSpec((t, wide), kmap), pl.BlockSpec((t, wide), vmap_),
                pl.BlockSpec((pairs, 2, LANES), lambda g, s, qt, kt, wt: (g % ng, 0, 0)), kx_spec]
    args = [p, p, p, jnp.asarray(qx), kx]
    if mode == "moba":
        km, = extra
        in_specs.append(pl.BlockSpec((1, km.shape[1], wide), lambda g, s, qt, kt, wt: (g // ng, 0, g % ng)))
        args.append(km)
    elif mode == "dil":
        d = (np.arange(win + 1)[:, None, None] * t + np.arange(t)[None, :, None] - np.arange(t)[None, None, :])
        lb = jnp.asarray(_log2_multiplicity(d))
        in_specs.append(pl.BlockSpec(lb.shape, lambda g, s, qt, kt, wt: (0, 0, 0)))
        args.append(lb)
    return pl.pallas_call(
        functools.partial(_flash_body, mode=mode, n_hp=n_hp, win=win),
        out_shape=jax.ShapeDtypeStruct((batch * seq_len, n_heads * HEAD_DIM), BF16),
        grid_spec=pltpu.PrefetchScalarGridSpec(
            num_scalar_prefetch=3,
            grid=(batch * ng, len(steps)),
            in_specs=in_specs,
            out_specs=pl.BlockSpec((t, wide), omap),
            scratch_shapes=[pltpu.VMEM((2 * pairs, t, LANES), BF16), pltpu.VMEM((2 * pairs, t, LANES), F32),
                            pltpu.VMEM((2 * pairs, t, LANES), F32)]),
        compiler_params=_cparams("parallel", "arbitrary"),
        name="prompt_attention_" + mode,
    )(qi_tab, ki_tab, jnp.asarray(w_eff), *args)


def _sample_block_means_body(pt_ref, *refs, group, pages_per_block):
    k_refs, o_ref = refs[:group], refs[group]
    s = pl.program_id(1)

    @pl.when(s == 0)
    def _():
        o_ref[...] = jnp.zeros_like(o_ref)

    lane = lax.broadcasted_iota(jnp.int32, o_ref.shape[1:], 2)
    for g in range(group):
        blk = (s * group + g) // pages_per_block
        part = jnp.sum(k_refs[g][0], axis=-1, keepdims=True) * (1.0 / MOBA_BLOCK)
        o_ref[0] = o_ref[0] + jnp.where(lane == blk, part, 0.0)


def sample_block_means(pool_t, page_table):
    db, n_pages = page_table.shape
    _, n_heads, _, page = pool_t.shape
    group = SAMPLE_PAGE_GROUP
    return pl.pallas_call(
        functools.partial(_sample_block_means_body, group=group, pages_per_block=MOBA_BLOCK // page),
        out_shape=jax.ShapeDtypeStruct((db, n_heads, HEAD_DIM, LANES), F32),
        grid_spec=pltpu.PrefetchScalarGridSpec(
            num_scalar_prefetch=1,
            grid=(db, n_pages // group),
            in_specs=[pl.BlockSpec((1, n_heads, HEAD_DIM, page),
                                   functools.partial(lambda b, s, pt, g: (pt[b * n_pages + s * group + g], 0, 0, 0), g=g))
                      for g in range(group)],
            out_specs=pl.BlockSpec((1, n_heads, HEAD_DIM, LANES), lambda b, s, pt: (b, 0, 0, 0))),
        compiler_params=_cparams("parallel", "arbitrary"),
        name="sample_block_means",
    )(page_table.reshape(-1), *([pool_t] * group))


def _sample_body(pt_ref, q_ref, kn_ref, vn_ref, *rest, mode, n_heads, n_steps, group, page, pages_per_block, paged):
    n_pool = group if paged else 1
    kp, vp, rest = rest[:n_pool], rest[n_pool:2 * n_pool], rest[2 * n_pool:]
    if mode == "moba":
        bias_ref, biasn_ref, kms_ref, o_ref, qbd_sc, kn_sc, vn_sc, acc_sc, m_sc, l_sc, sel_sc = rest
    elif mode == "fox":
        lfp, rest = rest[:group], rest[group:]
        lfn_ref, o_ref, qbd_sc, kn_sc, vn_sc, acc_sc, m_sc, l_sc, ncb_sc, run_sc = rest
    else:
        bias_ref, biasn_ref, mult_ref, multn_ref, o_ref, qbd_sc, kn_sc, vn_sc, acc_sc, m_sc, l_sc = rest
    s = pl.program_id(1)
    nq = q_ref.shape[0]
    rows, width = acc_sc.shape
    lane = lax.broadcasted_iota(jnp.int32, (rows, LANES), 1)
    row_q = lax.broadcasted_iota(jnp.int32, (rows, LANES), 0) % nq
    col_head = lax.broadcasted_iota(jnp.int32, (nq, width), 1) // HEAD_DIM

    def attend(scores, pv, bias, ok, mult):
        logit = scores + bias
        if ok is not None:
            logit = jnp.where(ok, logit, NEG)
        m_old = m_sc[...]
        m_new = jnp.maximum(m_old, jnp.max(logit, axis=-1, keepdims=True))
        alpha = jnp.exp(m_old - m_new)
        p = jnp.exp(logit - _lane_tile(m_new, logit.shape[1] // LANES))
        if mult is not None:
            p = p * mult
        l_sc[...] = alpha * l_sc[...] + jnp.sum(p, axis=-1, keepdims=True)
        acc_sc[...] = _lane_tile(alpha, width // LANES) * acc_sc[...] + pv(p.astype(BF16))
        m_sc[...] = m_new

    @pl.when(s == 0)
    def _():
        acc_sc[...] = jnp.zeros_like(acc_sc)
        l_sc[...] = jnp.zeros_like(l_sc)
        m_sc[...] = jnp.full_like(m_sc, NEG)
        kn_sc[...] = jnp.zeros_like(kn_sc)
        vn_sc[...] = jnp.zeros_like(vn_sc)
        kn_sc[0:nq, :] = kn_ref[...].astype(BF16)
        vn_sc[0:nq, :] = vn_ref[...].astype(BF16)
        q = q_ref[...]
        for h in range(n_heads):
            qbd_sc[h * nq:(h + 1) * nq, :] = jnp.where(col_head == h, q * SCALE, 0.0).astype(BF16)
        causal = lane <= row_q
        scores = lax.dot_general(qbd_sc[...], kn_sc[...], (((1,), (1,)), ((), ())), preferred_element_type=F32)
        pv_new = lambda p: jnp.dot(p, vn_sc[...], preferred_element_type=F32)
        if mode == "moba":
            gate = jnp.concatenate(
                [jnp.dot(q[:, h * HEAD_DIM:(h + 1) * HEAD_DIM], kms_ref[0, h],
                         preferred_element_type=F32, precision=HIGHEST) for h in range(n_heads)], axis=0)
            nb = n_steps * group // pages_per_block
            sel_sc[...] = _top_blocks(gate, float(nb), float(nb), LANES)
            attend(scores, pv_new, biasn_ref[...], causal, None)
        elif mode == "fox":
            lfn = lfn_ref[...]
            tri = (lax.broadcasted_iota(jnp.int32, (nq, nq), 1)
                   <= lax.broadcasted_iota(jnp.int32, (nq, nq), 0)).astype(F32)
            newcum = jnp.dot(tri, lfn, preferred_element_type=F32, precision=HIGHEST)
            eye = lane[:nq] == row_q[:nq]
            ncb, nrow = [], []
            for h in range(n_heads):
                col = jnp.broadcast_to(newcum[:, h:h + 1], (nq, LANES))
                ncb.append(col)
                nrow.append(jnp.broadcast_to(jnp.sum(jnp.where(eye, col, 0.0), axis=0, keepdims=True), (nq, LANES)))
            ncb = jnp.concatenate(ncb, axis=0)
            ncb_sc[...] = ncb
            run_sc[...] = jnp.zeros_like(run_sc)
            attend(scores, pv_new, ncb - jnp.concatenate(nrow, axis=0), causal, None)
        else:
            multn = multn_ref[...]
            attend(scores, pv_new, biasn_ref[...], multn > 0.0, multn)

    if paged:
        k_t = jnp.concatenate([kp[g][0].reshape(width, page) for g in range(group)], axis=1).astype(BF16)
        v_t = jnp.concatenate([vp[g][0].reshape(width, page) for g in range(group)], axis=1).astype(BF16)
    else:
        k_t = kp[0][0].reshape(width, group * page).astype(BF16)
        v_t = vp[0][0].reshape(width, group * page).astype(BF16)
    scores = jnp.dot(qbd_sc[...], k_t, preferred_element_type=F32)
    pv = lambda p: lax.dot_general(p, v_t, (((1,), (1,)), ((), ())), preferred_element_type=F32)
    if mode == "moba":
        hidden = []
        for g in range(group):
            blk = (s * group + g) // pages_per_block
            vis = jnp.max(jnp.where(lane == blk, sel_sc[...], 0.0), axis=-1, keepdims=True)
            hidden.append(jnp.broadcast_to(jnp.where(vis > 0.5, 0.0, NEG), (rows, page)))
        attend(scores, pv, bias_ref[0] + jnp.concatenate(hidden, axis=1), None, None)
    elif mode == "fox":
        lft = jnp.concatenate([lfp[g][0] for g in range(group)], axis=0)
        upper = (lax.broadcasted_iota(jnp.int32, (page, page), 0)
                 > lax.broadcasted_iota(jnp.int32, (page, page), 1)).astype(F32)
        inside = jnp.dot(lft, upper, preferred_element_type=F32, precision=HIGHEST)
        totals = jnp.sum(lft, axis=-1, keepdims=True)
        run = run_sc[...]
        pieces = []
        for g in range(group):
            after = inside[g * n_heads:(g + 1) * n_heads] + run
            pieces.append(jnp.concatenate(
                [jnp.broadcast_to(after[h:h + 1, :], (nq, page)) for h in range(n_heads)], axis=0))
            run = run + totals[g * n_heads:(g + 1) * n_heads]
        run_sc[...] = run
        attend(scores, pv, _lane_tile(ncb_sc[...], group) + jnp.concatenate(pieces, axis=1), None, None)
    else:
        mult = mult_ref[0]
        attend(scores, pv, bias_ref[0], mult > 0.0, mult)

    @pl.when(s == n_steps - 1)
    def _():
        out = acc_sc[...] / _lane_tile(l_sc[...], width // LANES)
        o_ref[...] = sum(jnp.where(col_head == h, out[h * nq:(h + 1) * nq, :], 0.0) for h in range(n_heads))


def _multiplicity(d):
    m = np.zeros(d.shape, np.float32)
    for window, dil in DILATED_BRANCHES:
        m += ((d >= 0) & (d <= window) & (d % dil == 0)).astype(np.float32)
    return m


def _sample_tables(past_len, nq, n_heads, n_steps, span):
    d = (past_len + np.arange(nq)[None, :, None]) - (np.arange(n_steps)[:, None, None] * span
                                                     + np.arange(span)[None, None, :])
    dn = np.arange(nq)[:, None] - np.arange(LANES)[None, :]
    slopes = np.repeat(np.asarray(_alibi_slopes(n_heads), np.float64), nq)[None, :, None]
    tile = lambda a: np.tile(a, (1, n_heads, 1))
    d, dn = tile(d), tile(dn[None])
    return ((-slopes * d).astype(np.float32), (-slopes * dn)[0].astype(np.float32),
            _multiplicity(d), _multiplicity(dn)[0])


def sample_attention(p, mode, row0, db, nq, n_heads, q_blk, k_blk, v_blk, k_pool_t, v_pool_t, page_table, extra=()):
    width = n_heads * HEAD_DIM
    rows = n_heads * nq
    group = SAMPLE_PAGE_GROUP
    reverse = mode == "fox"
    paged = page_table is not None
    page = k_pool_t.shape[3] if paged else LANES
    assert page == LANES
    n_pages = page_table.shape[1] if paged else k_pool_t.shape[3] // page
    pt = page_table.reshape(-1) if paged else jnp.zeros((1,), jnp.int32)
    n_steps = n_pages // group
    span = group * page
    rb0 = row0 // nq

    def pg(s, g):
        j = s * group + g
        return (n_pages - 1 - j) if reverse else j

    def pool_spec(g):
        return pl.BlockSpec((1, n_heads, HEAD_DIM, page), lambda b, s, t: (t[b * n_pages + pg(s, g)], 0, 0, 0))

    if paged:
        pool_specs = [pool_spec(g) for g in range(group)]
        pools = lambda a: [a] * group
    else:
        pool_specs = [pl.BlockSpec((1, n_heads, HEAD_DIM, span), lambda b, s, t: (b, 0, 0, s))]
        pools = lambda a: [a]
    bias, biasn, mult, multn = [jnp.asarray(a) for a in _sample_tables(n_pages * page, nq, n_heads, n_steps, span)]
    table_spec = pl.BlockSpec((1, rows, span), lambda b, s, t: (s, 0, 0))
    new_spec = pl.BlockSpec((rows, LANES), lambda b, s, t: (0, 0))
    in_specs = [pl.BlockSpec((nq, width), lambda b, s, t: (rb0 + b, q_blk)),
                pl.BlockSpec((nq, width), lambda b, s, t: (rb0 + b, k_blk)),
                pl.BlockSpec((nq, width), lambda b, s, t: (rb0 + b, v_blk))]
    in_specs += pool_specs * 2
    args = [p, p, p] + pools(k_pool_t) + pools(v_pool_t)
    scratch = [pltpu.VMEM((rows, width), BF16), pltpu.VMEM((LANES, width), BF16), pltpu.VMEM((LANES, width), BF16),
               pltpu.VMEM((rows, width), F32), pltpu.VMEM((rows, LANES), F32), pltpu.VMEM((rows, LANES), F32)]
    if mode == "moba":
        kms, = extra
        in_specs += [table_spec, new_spec, pl.BlockSpec((1,) + kms.shape[1:], lambda b, s, t: (b, 0, 0, 0))]
        args += [bias, biasn, kms]
        scratch.append(pltpu.VMEM((rows, LANES), F32))
    elif mode == "fox":
        lf_pool_t, logf = extra
        in_specs += [pl.BlockSpec((1, n_heads, page),
                                  functools.partial(lambda b, s, t, g: (t[b * n_pages + pg(s, g)], 0, 0), g=g))
                     for g in range(group)]
        in_specs.append(pl.BlockSpec((nq, LANES), lambda b, s, t: (rb0 + b, 0)))
        args += [lf_pool_t] * group + [logf]
        scratch += [pltpu.VMEM((rows, LANES), F32), pltpu.VMEM((n_heads, LANES), F32)]
    else:
        in_specs += [table_spec, new_spec, table_spec, new_spec]
        args += [bias, biasn, mult, multn]
    return pl.pallas_call(
        functools.partial(_sample_body, mode=mode, n_heads=n_heads, n_steps=n_steps, group=group, page=page,
                          pages_per_block=MOBA_BLOCK // page, paged=paged),
        out_shape=jax.ShapeDtypeStruct((db * nq, width), F32),
        grid_spec=pltpu.PrefetchScalarGridSpec(
            num_scalar_prefetch=1,
            grid=(db, n_steps),
            in_specs=in_specs,
            out_specs=pl.BlockSpec((nq, width), lambda b, s, t: (b, 0)),
            scratch_shapes=scratch),
        compiler_params=_cparams("parallel", "arbitrary"),
        name="sample_attention_" + mode,
    )(pt, *args)


RANK_NONE = 127.0


def _top_rows(x, k, with_rank=False):
    vals = []
    rank = jnp.full(x.shape, RANK_NONE, F32)
    for r in range(k):
        m = jnp.max(x, axis=0, keepdims=True)
        vals.append(m)
        hit = x == m
        if with_rank:
            rank = jnp.where(hit, float(r), rank)
        x = jnp.where(hit, -jnp.inf, x)
    return (vals, rank) if with_rank else vals


def _bf16_twice(x):
    bits = lax.bitcast_convert_type(x.astype(BF16).astype(F32), jnp.uint32)
    return bits | (bits >> 16)


def _peer_scores_body(ht_ref, wq_ref, sk_ref, n1_ref, r2_ref, e1_ref, e2_ref):
    half = PEER_KEYS
    k1 = PEER_TOPK + 1
    qt = jnp.dot(wq_ref[...], ht_ref[...], preferred_element_type=F32)
    s1 = jnp.dot(sk_ref[0, 0], qt[:half], preferred_element_type=F32, precision=HIGHEST)
    s2 = jnp.dot(sk_ref[0, 1], qt[half:], preferred_element_type=F32, precision=HIGHEST)
    t1 = _top_rows(s1, k1)
    t2, rank2 = _top_rows(s2, k1, with_rank=True)
    cand = [t1[i] + t2[j] for i in range(k1) for j in range(k1) if (i + 1) * (j + 1) <= k1]
    pad = -len(cand) % SUBLANES
    cand = jnp.concatenate(cand + [jnp.full_like(t1[0], -jnp.inf)] * pad, axis=0)
    top = _top_rows(cand, k1)
    z = sum(jnp.exp(v - top[0]) for v in top[:PEER_TOPK])
    tau = 0.5 * (top[PEER_TOPK - 1] + top[PEER_TOPK])
    need = tau - s1
    n1 = jnp.zeros_like(s1)
    for j in range(k1):
        n1 = jnp.where(t2[j] >= need, float(j + 1), n1)
    n1_ref[0] = _bf16_twice(n1)
    r2_ref[0] = pltpu.bitcast(rank2.astype(BF16), jnp.uint32)
    e1_ref[0] = _bf16_twice(jnp.exp(s1 - t1[0]) / z)
    e2_ref[0] = pltpu.bitcast(jnp.exp(s2 - t2[0]).astype(BF16), jnp.uint32)


def peer_scores(ht, wq_t, subkeys):
    d, t = ht.shape
    tt = PEER_TOKEN_TILE
    nh = PEER_HEADS
    dq = wq_t.shape[0] // nh
    big = jax.ShapeDtypeStruct((nh, PEER_KEYS, t), jnp.uint32)
    small = jax.ShapeDtypeStruct((nh, PEER_KEYS // 2, t), jnp.uint32)
    bspec = pl.BlockSpec((1, PEER_KEYS, tt), lambda i, h: (h, 0, i))
    pspec = pl.BlockSpec((1, PEER_KEYS // 2, tt), lambda i, h: (h, 0, i))
    return pl.pallas_call(
        _peer_scores_body,
        out_shape=(big, small, big, small),
        grid=(t // tt, nh),
        in_specs=[pl.BlockSpec((d, tt), lambda i, h: (0, i)),
                  pl.BlockSpec((dq, d), lambda i, h: (h, 0)),
                  pl.BlockSpec((1,) + subkeys.shape[1:], lambda i, h: (h, 0, 0, 0))],
        out_specs=(bspec, pspec, bspec, pspec),
        compiler_params=_cparams("parallel", "arbitrary"),
        name="peer_scores",
    )(ht, wq_t, subkeys)


def _gelu(x):
    return 0.5 * x * (1.0 + lax.erf(x * (1.0 / math.sqrt(2.0))))


def _peer_experts_body(ht_ref, u_ref, vt_ref, n1_ref, r2_ref, e1_ref, e2_ref, x_ref, g_ref,
                       xn_ref, hn_ref, acc_sc, act_sc, w_sc):
    j = pl.program_id(1)
    te = u_ref.shape[0]
    tt = ht_ref.shape[1]
    groups = te // PEER_KEYS

    @pl.when(j == 0)
    def _():
        acc_sc[...] = jnp.zeros_like(acc_sc)

    act_sc[...] = _gelu(jnp.dot(u_ref[...], ht_ref[...], preferred_element_type=F32)).astype(BF16)
    n_tc = tt // LANES
    packed_rows = (PEER_KEYS // 2, LANES)
    for al in range(groups):
        rows = slice(al * PEER_KEYS, (al + 1) * PEER_KEYS)
        coef = [jnp.zeros((PEER_KEYS, LANES), BF16) for _ in range(n_tc)]
        for h in range(PEER_HEADS):
            n1_row = n1_ref[h, al:al + 1, :]
            e1_row = e1_ref[h, al:al + 1, :]
            for tc in range(n_tc):
                cols = slice(tc * LANES, (tc + 1) * LANES)
                n1 = pltpu.bitcast(jnp.broadcast_to(n1_row[:, cols], packed_rows), BF16)
                e1 = pltpu.bitcast(jnp.broadcast_to(e1_row[:, cols], packed_rows), BF16)
                hit = pltpu.bitcast(r2_ref[h, :, cols], BF16) < n1
                e2 = pltpu.bitcast(e2_ref[h, :, cols], BF16)
                coef[tc] = coef[tc] + jnp.where(hit, e2, jnp.zeros((), BF16)) * e1
        for tc in range(n_tc):
            cols = slice(tc * LANES, (tc + 1) * LANES)
            w_sc[rows, cols] = coef[tc] * act_sc[rows, cols]
    acc_sc[...] += jnp.dot(vt_ref[...], w_sc[...], preferred_element_type=F32)

    @pl.when(j == pl.num_programs(1) - 1)
    def _():
        xn = x_ref[...] + acc_sc[...].T
        xn_ref[...] = xn
        hn_ref[...] = _rms(xn, g_ref[...]).astype(hn_ref.dtype)


def peer_experts(x, ht, u, v_t, scores, g_next, norm_dtype):
    t, d = x.shape
    tt = PEER_TOKEN_TILE
    te = PEER_EXPERT_TILE
    n_exp = u.shape[0]
    sspec = pl.BlockSpec((PEER_HEADS, te // PEER_KEYS, tt), lambda i, j: (0, j, i))
    pspec = pl.BlockSpec((PEER_HEADS, PEER_KEYS // 2, tt), lambda i, j: (0, 0, i))
    return pl.pallas_call(
        _peer_experts_body,
        out_shape=(jax.ShapeDtypeStruct((t, d), F32), jax.ShapeDtypeStruct((t, d), norm_dtype)),
        grid=(t // tt, n_exp // te),
        in_specs=[pl.BlockSpec((d, tt), lambda i, j: (0, i)),
                  pl.BlockSpec((te, d), lambda i, j: (j, 0)),
                  pl.BlockSpec((d, te), lambda i, j: (0, j)),
                  sspec, pspec, sspec, pspec,
                  pl.BlockSpec((tt, d), lambda i, j: (i, 0)),
                  pl.BlockSpec((1, d), lambda i, j: (0, 0))],
        out_specs=(pl.BlockSpec((tt, d), lambda i, j: (i, 0)),
                   pl.BlockSpec((tt, d), lambda i, j: (i, 0))),
        scratch_shapes=[pltpu.VMEM((d, tt), F32), pltpu.VMEM((te, tt), BF16), pltpu.VMEM((te, tt), BF16)],
        compiler_params=_cparams("parallel", "arbitrary"),
        name="peer_experts",
    )(ht, u, v_t, *scores, x, g_next.reshape(1, d))


def _head_major_body(cb_ref, x_ref, o_ref):
    n_heads = o_ref.shape[2]
    o_ref[0, 0] = x_ref[...].T.reshape(n_heads, HEAD_DIM, x_ref.shape[0])


def head_major(p, col_blocks, batch, seq_len, row_start, rows, n_heads):
    width = n_heads * HEAD_DIM
    t = ROW_TILE
    assert rows % t == 0 and row_start % t == 0 and seq_len % t == 0
    return pl.pallas_call(
        _head_major_body,
        out_shape=jax.ShapeDtypeStruct((len(col_blocks), batch, n_heads, HEAD_DIM, rows), F32),
        grid_spec=pltpu.PrefetchScalarGridSpec(
            num_scalar_prefetch=1,
            grid=(len(col_blocks), batch, rows // t),
            in_specs=[pl.BlockSpec((t, width), lambda g, b, i, cb: ((b * seq_len + row_start) // t + i, cb[g]))],
            out_specs=pl.BlockSpec((1, 1, n_heads, HEAD_DIM, t), lambda g, b, i, cb: (g, b, 0, 0, i))),
        compiler_params=_cparams("parallel", "parallel", "parallel"),
        name="head_major",
    )(jnp.asarray(col_blocks, jnp.int32), p)


def _heads(a, n):
    return a.reshape(a.shape[:-1] + (n, HEAD_DIM))


def kernel(x_prompt, x_sample, cache_moba_k, cache_moba_v, cache_fox_k, cache_fox_v, cache_fox_logf,
           state_swa_k, state_swa_v, page_table, norm_gain, final_gain, w_in_even, b_forget, w_out_even,
           w_in_odd, w_out_odd, peer_w_query, peer_subkeys, peer_u, peer_v):
    b, s, d = x_prompt.shape
    db, ds, _ = x_sample.shape
    tp = b * s
    depth = norm_gain.shape[0]
    n_pool, page = cache_moba_k.shape[1:3]
    past_len = page_table.shape[1] * page
    win_buf = state_swa_k.shape[2]
    keep_p = min(WIN_MAX, s)
    assert win_buf == past_len and past_len % MOBA_BLOCK == 0 and MOBA_BLOCK % page == 0
    assert s % ATT_TILE == 0 and (tp + db * ds) % ROW_TILE == 0 and tp % ds == 0

    x = jnp.concatenate([x_prompt.reshape(tp, d), x_sample.reshape(db * ds, d)], axis=0)
    h = rmsnorm_rows(x, norm_gain[0, 0], BF16)
    new = {}
    for layer in range(depth):
        li = layer // 2
        if layer % 2 == 0:
            n_in = w_in_even.shape[2]
            n_pad = -n_in % (5 * LANES)
            w_in = jnp.pad(w_in_even[li], ((0, 0), (0, n_pad))).astype(BF16)
            p = matmul(h, w_in, (n_in + n_pad) // 5)
            logf, fox_kx = fox_gate(p, 3 * (D_A + D_B) // LANES, b_forget[li], s)
            km = moba_block_means(p, 1, b, s)
            oa_p = prompt_attention(p, "moba", b, s, N_HEADS_A, 0, D_A // LANES, 2 * D_A // LANES, (km,))
            fb = 3 * D_A // LANES
            ob_p = prompt_attention(p, "fox", b, s, N_HEADS_B, fb, fb + D_B // LANES, fb + 2 * D_B // LANES,
                                    (fox_kx,))
            pool = lambda c: jnp.transpose(c[li], (0, 2, 3, 1))
            ka_pool, va_pool = pool(cache_moba_k), pool(cache_moba_v)
            kb_pool, vb_pool = pool(cache_fox_k), pool(cache_fox_v)
            kms = sample_block_means(ka_pool, page_table)
            oa_s = sample_attention(p, "moba", tp, db, ds, N_HEADS_A, 0, 1, 2, ka_pool, va_pool, page_table, (kms,))
            lf_pool_t = jnp.swapaxes(cache_fox_logf[li], 1, 2)
            ob_s = sample_attention(p, "fox", tp, db, ds, N_HEADS_B, 3, 4, 5, kb_pool, vb_pool, page_table,
                                    (lf_pool_t, logf))
            w_out = w_out_even[li].astype(BF16)
            pairs = [(jnp.concatenate([oa_p, oa_s.astype(BF16)], axis=0), w_out[:D_A]),
                     (jnp.concatenate([ob_p, ob_s.astype(BF16)], axis=0), w_out[D_A:])]
            assert N_HEADS_A == N_HEADS_B
            names = ("mk", "mv", "fk", "fv")
            col_blocks = (1, 2, 4, 5)
            stored = head_major(p, col_blocks, b, s, 0, s, N_HEADS_A)
            for g, name in enumerate(names):
                cols = p[tp:, col_blocks[g] * D_A:(col_blocks[g] + 1) * D_A]
                new.setdefault("p_" + name, []).append(jnp.transpose(stored[g], (0, 3, 1, 2)))
                new.setdefault("s_" + name, []).append(_heads(cols.reshape(db, ds, -1), N_HEADS_A))
            new.setdefault("p_fl", []).append(logf[:tp, :N_HEADS_B].reshape(b, s, N_HEADS_B))
            new.setdefault("s_fl", []).append(logf[tp:, :N_HEADS_B].reshape(db, ds, N_HEADS_B))
        else:
            p = matmul(h, w_in_odd[li].astype(BF16), D_C // 2)
            nb = D_C // LANES
            oc_p = prompt_attention(p, "dil", b, s, N_HEADS_C, 0, nb, 2 * nb)
            kc_pool = jnp.transpose(state_swa_k[li], (0, 2, 3, 1))
            vc_pool = jnp.transpose(state_swa_v[li], (0, 2, 3, 1))
            oc_s = sample_attention(p, "dil", tp, db, ds, N_HEADS_C, 0, 1, 2, kc_pool, vc_pool, None)
            pairs = [(jnp.concatenate([oc_p, oc_s.astype(BF16)], axis=0), w_out_odd[li].astype(BF16))]
            stored = head_major(p, (1, 2), b, s, s - keep_p, keep_p, N_HEADS_C)
            for g, name in enumerate(("sk", "sv")):
                cols = p[tp:, (g + 1) * D_C:(g + 2) * D_C]
                new.setdefault("p_" + name, []).append(jnp.transpose(stored[g], (0, 3, 1, 2)))
                new.setdefault("s_" + name, []).append(_heads(cols.reshape(db, ds, -1), N_HEADS_C))
        x, _, h_t = resid_norm(x, pairs, norm_gain[layer, 1])
        scores = peer_scores(h_t, peer_w_query[layer].T.astype(BF16), peer_subkeys[layer])
        last = layer == depth - 1
        g_next = final_gain if last else norm_gain[layer + 1, 0]
        x, h = peer_experts(x, h_t, peer_u[layer].astype(BF16), peer_v[layer].T.astype(BF16), scores,
                            g_next, F32 if last else BF16)
    y = h
    order = ["p_mk", "p_mv", "p_fk", "p_fv", "p_fl", "p_sk", "p_sv",
             "s_mk", "s_mv", "s_fk", "s_fv", "s_fl", "s_sk", "s_sv"]
    return (y[:tp].reshape(b, s, d), y[tp:].reshape(db, ds, d)) + tuple(jnp.stack(new[k]) for k in order)
```

```python
import functools
import math

import numpy as np
import jax
import jax.numpy as jnp
from jax import lax
from jax.experimental import pallas as pl
from jax.experimental.pallas import tpu as pltpu

F32 = jnp.float32
BF16 = jnp.bfloat16
HIGHEST = lax.Precision.HIGHEST

LANES = 128
SUBLANES = 8
VMEM_LIMIT_BYTES = 56 * 1024 * 1024

HEAD_DIM = 64
N_HEADS_A = 8
N_HEADS_B = 8
N_HEADS_C = 16
D_A = N_HEADS_A * HEAD_DIM
D_B = N_HEADS_B * HEAD_DIM
D_C = N_HEADS_C * HEAD_DIM
MOBA_BLOCK = 256
MOBA_TOPK = 3
DILATED_BRANCHES = ((128, 1), (512, 4), (2048, 16))
WIN_MAX = 2048
PEER_HEADS = 8
PEER_KEYS = 128
PEER_TOPK = 16
RMS_EPS = 1e-6
NEG = -1e30
SCALE = HEAD_DIM ** -0.5
LOG2E = 1.0 / math.log(2.0)

ATT_TILE = 512
ATT_HEAD_PAIRS = 4
GATE_TILE = 256
ROW_TILE = 512
PEER_TOKEN_TILE = 512
PEER_EXPERT_TILE = 1024
SAMPLE_PAGED_GROUP = 16
SAMPLE_SPAN_GROUP = 8

X_POS_HI = 0
X_POS_LO = 3
X_FORGET = 0
X_BLOCK = 8
POS_SPLIT = 16
N_PIECES = 3


def _cparams(*sem):
    return pltpu.CompilerParams(dimension_semantics=sem, vmem_limit_bytes=VMEM_LIMIT_BYTES)


def _alibi_slopes(n):
    return [2.0 ** (-8.0 * (i + 1) / n) for i in range(n)]


def _bf16_pieces_np(x):
    x = np.asarray(x, np.float32)
    out = []
    for _ in range(N_PIECES):
        p = np.asarray(np.asarray(x, dtype=BF16), np.float32)
        out.append(p)
        x = x - p
    return out


def _rms(x, g):
    return x * lax.rsqrt(jnp.mean(x * x, axis=-1, keepdims=True) + RMS_EPS) * g


def _lane_tile(x, n):
    return x if n == 1 else jnp.concatenate([x] * n, axis=1)


def _rms_body(x_ref, g_ref, h_ref):
    h_ref[...] = _rms(x_ref[...], g_ref[...]).astype(h_ref.dtype)


def rmsnorm_rows(x, g, out_dtype):
    t, d = x.shape
    return pl.pallas_call(
        _rms_body,
        out_shape=jax.ShapeDtypeStruct((t, d), out_dtype),
        grid=(t // ROW_TILE,),
        in_specs=[pl.BlockSpec((ROW_TILE, d), lambda i: (i, 0)),
                  pl.BlockSpec((1, d), lambda i: (0, 0))],
        out_specs=pl.BlockSpec((ROW_TILE, d), lambda i: (i, 0)),
        compiler_params=_cparams("parallel"),
        name="rmsnorm_rows",
    )(x, g.reshape(1, d))


def _mm_body(h_ref, w_ref, o_ref):
    o_ref[...] = jnp.dot(h_ref[...], w_ref[...], preferred_element_type=F32)


def matmul(h, w, tn):
    t, k = h.shape
    n = w.shape[1]
    return pl.pallas_call(
        _mm_body,
        out_shape=jax.ShapeDtypeStruct((t, n), F32),
        grid=(t // ROW_TILE, n // tn),
        in_specs=[pl.BlockSpec((ROW_TILE, k), lambda i, j: (i, 0)),
                  pl.BlockSpec((k, tn), lambda i, j: (0, j))],
        out_specs=pl.BlockSpec((ROW_TILE, tn), lambda i, j: (i, j)),
        compiler_params=_cparams("parallel", "arbitrary"),
        name="matmul",
    )(h, w)


def _resid_norm_body(*refs, n_pairs):
    x_ref = refs[0]
    g_ref = refs[1 + 2 * n_pairs]
    xn_ref, h_ref, ht_ref = refs[2 + 2 * n_pairs:]
    acc = x_ref[...]
    for p in range(n_pairs):
        acc = acc + jnp.dot(refs[1 + 2 * p][...], refs[2 + 2 * p][...], preferred_element_type=F32)
    xn_ref[...] = acc
    h = _rms(acc, g_ref[...])
    h_ref[...] = h.astype(h_ref.dtype)
    ht_ref[...] = h.T.astype(ht_ref.dtype)


def resid_norm(x, pairs, g):
    t, d = x.shape
    args = [x]
    in_specs = [pl.BlockSpec((ROW_TILE, d), lambda i: (i, 0))]
    for a, w in pairs:
        args += [a, w]
        in_specs += [pl.BlockSpec((ROW_TILE, a.shape[1]), lambda i: (i, 0)),
                     pl.BlockSpec(w.shape, lambda i: (0, 0))]
    args.append(g.reshape(1, d))
    in_specs.append(pl.BlockSpec((1, d), lambda i: (0, 0)))
    return pl.pallas_call(
        functools.partial(_resid_norm_body, n_pairs=len(pairs)),
        out_shape=(jax.ShapeDtypeStruct((t, d), F32),
                   jax.ShapeDtypeStruct((t, d), BF16),
                   jax.ShapeDtypeStruct((d, t), BF16)),
        grid=(t // ROW_TILE,),
        in_specs=in_specs,
        out_specs=(pl.BlockSpec((ROW_TILE, d), lambda i: (i, 0)),
                   pl.BlockSpec((ROW_TILE, d), lambda i: (i, 0)),
                   pl.BlockSpec((d, ROW_TILE), lambda i: (0, i))),
        compiler_params=_cparams("parallel"),
        name="resid_norm",
    )(*args)


def _log_sigmoid(x):
    return -(jnp.maximum(-x, 0.0) + jnp.log1p(jnp.exp(-jnp.abs(x))))


def _fox_gate_body(p_ref, b_ref, place_ref, lf_ref, kx_ref, carry_sc, *, tiles_per_seq):
    i = pl.program_id(0)
    t = p_ref.shape[0]

    @pl.when(i % tiles_per_seq == 0)
    def _():
        carry_sc[...] = jnp.zeros_like(carry_sc)

    lf = _log_sigmoid(p_ref[...] + b_ref[...])
    lf_ref[...] = lf
    row = lax.broadcasted_iota(jnp.int32, (t, t), 0)
    col = lax.broadcasted_iota(jnp.int32, (t, t), 1)
    tri = (col <= row).astype(F32)
    c = jnp.dot(tri, lf, preferred_element_type=F32, precision=HIGHEST) + carry_sc[0:1, :]
    carry_sc[...] = jnp.broadcast_to(c[t - 1:t, :], carry_sc.shape)
    rest = c * LOG2E
    pieces = []
    for _ in range(N_PIECES):
        piece = rest.astype(BF16)
        pieces.append(piece)
        rest = rest - piece.astype(F32)
    for hp in range(kx_ref.shape[0]):
        moved = sum(jnp.dot(pieces[k], place_ref[hp, k], preferred_element_type=F32) for k in range(N_PIECES))
        kx_ref[hp] = (-moved).astype(BF16)


def fox_gate(p, col_block, b_forget, seq_len):
    t = p.shape[0]
    n_heads = b_forget.shape[0]
    n_hp = n_heads // 2
    b = jnp.zeros((1, LANES), F32).at[0, :n_heads].set(b_forget)
    place = np.zeros((n_hp, N_PIECES, LANES, LANES), np.float32)
    for hp in range(n_hp):
        for k in range(N_PIECES):
            place[hp, k, 2 * hp, HEAD_DIM + X_FORGET + k] = 1.0
            place[hp, k, 2 * hp + 1, X_FORGET + k] = 1.0
    return pl.pallas_call(
        functools.partial(_fox_gate_body, tiles_per_seq=seq_len // GATE_TILE),
        out_shape=(jax.ShapeDtypeStruct((t, LANES), F32),
                   jax.ShapeDtypeStruct((n_hp, t, LANES), BF16)),
        grid=(t // GATE_TILE,),
        in_specs=[pl.BlockSpec((GATE_TILE, LANES), lambda i: (i, col_block)),
                  pl.BlockSpec((1, LANES), lambda i: (0, 0)),
                  pl.BlockSpec(place.shape, lambda i: (0, 0, 0, 0))],
        out_specs=(pl.BlockSpec((GATE_TILE, LANES), lambda i: (i, 0)),
                   pl.BlockSpec((n_hp, GATE_TILE, LANES), lambda i: (0, i, 0))),
        scratch_shapes=[pltpu.VMEM((SUBLANES, LANES), F32)],
        compiler_params=_cparams("arbitrary"),
        name="fox_gate",
    )(p, b, jnp.asarray(place, BF16))


def _kmean_body(k_ref, o_ref):
    j = pl.program_id(1)
    o_ref[0, pl.ds(j, 1), :] = jnp.sum(k_ref[...], axis=0, keepdims=True) * (1.0 / MOBA_BLOCK)


def moba_block_means(p, k_col_block, batch, seq_len):
    nb = seq_len // MOBA_BLOCK
    return pl.pallas_call(
        _kmean_body,
        out_shape=jax.ShapeDtypeStruct((batch, nb, D_A), F32),
        grid=(batch, nb),
        in_specs=[pl.BlockSpec((MOBA_BLOCK, D_A), lambda b, j: (b * nb + j, k_col_block))],
        out_specs=pl.BlockSpec((1, nb, D_A), lambda b, j: (b, 0, 0)),
        compiler_params=_cparams("parallel", "arbitrary"),
        name="moba_block_means",
    )(p)


def _top_blocks(gate, n_valid, own, n_blocks, axis=1):
    jidx = lax.broadcasted_iota(jnp.int32, gate.shape, axis).astype(F32)
    gate = jnp.where(jidx < n_valid, gate, NEG)
    sel = (jidx == own).astype(F32)
    for _ in range(MOBA_TOPK):
        mx = jnp.max(gate, axis=axis, keepdims=True)
        am = jnp.min(jnp.where(gate == mx, jidx, float(n_blocks)), axis=axis, keepdims=True)
        hit = jidx == am
        sel = jnp.where(jnp.logical_and(hit, am < n_valid), 1.0, sel)
        gate = jnp.where(hit, -jnp.inf, gate)
    return sel


def _flash_body(qi_tab, ki_tab, w_tab, q_ref, k_ref, v_ref, qx_ref, kx_ref, *rest, mode, n_hp, win):
    if mode == "moba":
        km_ref, o_ref, qm_sc, acc_sc, m_sc = rest
    elif mode == "fox":
        o_ref, qm_sc, acc_sc, m_sc = rest
    else:
        lb_ref, o_ref, qm_sc, acc_sc, m_sc = rest
    t = q_ref.shape[0]
    pairs = q_ref.shape[1] // LANES
    hp0 = (pl.program_id(0) % (n_hp // pairs)) * pairs
    s = pl.program_id(1)
    qi = qi_tab[s]
    ki = ki_tab[s]
    delta = qi - ki
    lane = lax.broadcasted_iota(jnp.int32, (t, LANES), 1)
    lo = lane < HEAD_DIM
    own_half = (lo, jnp.logical_not(lo))
    pair_lanes = [slice(pp * LANES, (pp + 1) * LANES) for pp in range(pairs)]

    @pl.when(ki == jnp.maximum(qi - win, 0))
    def _():
        acc_sc[...] = jnp.zeros_like(acc_sc)
        m_sc[...] = jnp.full_like(m_sc, NEG)
        for pp in range(pairs):
            q = q_ref[:, pair_lanes[pp]]
            qs = q * (SCALE * LOG2E)
            for h in range(2):
                qm = jnp.where(own_half[h], qs, qx_ref[pp, h:h + 1, :])
                if mode == "moba":
                    km = km_ref[0, :, pair_lanes[pp]]
                    nb = km.shape[0]
                    gate = lax.dot_general(km, jnp.where(own_half[h], q, 0.0), (((1,), (1,)), ((), ())),
                                           preferred_element_type=F32, precision=HIGHEST)
                    own_blk = (qi * (t // MOBA_BLOCK)
                               + lax.broadcasted_iota(jnp.int32, (1, t), 1) // MOBA_BLOCK).astype(F32)
                    hidden = (_top_blocks(gate, own_blk, own_blk, nb, axis=0) - 1.0) * (-NEG)
                    base = (1 - h) * HEAD_DIM + X_BLOCK
                    place = (lax.broadcasted_iota(jnp.int32, (nb, LANES), 1)
                             == lax.broadcasted_iota(jnp.int32, (nb, LANES), 0) + base).astype(BF16)
                    qm = qm + lax.dot_general(hidden.astype(BF16), place, (((0,), (0,)), ((), ())),
                                              preferred_element_type=F32)
                qm_sc[pp * 2 + h] = qm.astype(BF16)

    def step(causal):
        if mode != "fox":
            kx = kx_ref[...]
        if mode == "moba":
            key_blk = ki * (t // MOBA_BLOCK) + lax.broadcasted_iota(jnp.int32, (t, LANES), 0) // MOBA_BLOCK
            kx = jnp.where((lane % HEAD_DIM) == key_blk + X_BLOCK, 1.0, kx)
        if causal:
            keep = (lax.broadcasted_iota(jnp.int32, (t, t), 1) <= lax.broadcasted_iota(jnp.int32, (t, t), 0))
        for pp in range(pairs):
            k2 = k_ref[:, pair_lanes[pp]]
            v2 = v_ref[:, pair_lanes[pp]]
            if mode == "fox":
                kx = kx_ref[pp].astype(F32)
            for h in range(2):
                hs = pp * 2 + h
                k_aug = jnp.where(own_half[h], k2, kx).astype(BF16)
                sc = lax.dot_general(qm_sc[hs], k_aug, (((1,), (1,)), ((), ())), preferred_element_type=F32)
                if mode == "dil":
                    sc = sc + lb_ref[delta]
                if causal:
                    sc = jnp.where(keep, sc, NEG)
                off = 0.0 if mode == "fox" else w_tab[(hp0 + pp) * 2 + h] * (ki * t).astype(F32)
                m_old = m_sc[hs]
                m_new = jnp.maximum(m_old, jnp.max(sc, axis=-1, keepdims=True) + off)
                alpha = jnp.exp2(m_old - m_new)
                p = jnp.exp2(sc - _lane_tile(m_new - off, t // LANES))
                v_aug = jnp.where(own_half[h], v2, 1.0).astype(BF16)
                acc_sc[hs] = alpha * acc_sc[hs] + jnp.dot(p.astype(BF16), v_aug, preferred_element_type=F32)
                m_sc[hs] = m_new

    if mode == "dil":
        step(False)
    else:
        pl.when(delta == 0)(lambda: step(True))
        pl.when(delta != 0)(lambda: step(False))

    @pl.when(ki == qi)
    def _():
        for pp in range(pairs):
            a0 = acc_sc[pp * 2]
            a1 = acc_sc[pp * 2 + 1]
            o_ref[:, pair_lanes[pp]] = jnp.where(lo, a0 / a0[:, HEAD_DIM:HEAD_DIM + 1],
                                                 a1 / a1[:, 0:1]).astype(o_ref.dtype)


def _log2_multiplicity(d):
    m = np.zeros(d.shape, np.float64)
    for window, dil in DILATED_BRANCHES:
        m += ((d >= 0) & (d <= window) & (d % dil == 0))
    return np.where(m > 0, np.log2(np.maximum(m, 1.0)), NEG).astype(np.float32)


def _alibi_extras(n_heads, t):
    qx = np.zeros((n_heads, LANES), np.float32)
    w_eff = np.zeros((n_heads,), np.float32)
    for hg, slope in enumerate(_alibi_slopes(n_heads)):
        base = (1 - hg % 2) * HEAD_DIM
        for k, piece in enumerate(_bf16_pieces_np(slope * LOG2E)):
            qx[hg, base + X_POS_HI + k] = POS_SPLIT * piece
            qx[hg, base + X_POS_LO + k] = piece
            w_eff[hg] += piece
    col = np.arange(t)
    kx = np.zeros((t, LANES), np.float32)
    for base in (0, HEAD_DIM):
        for k in range(N_PIECES):
            kx[:, base + X_POS_HI + k] = col // POS_SPLIT
            kx[:, base + X_POS_LO + k] = col % POS_SPLIT
    return qx.reshape(n_heads // 2, 2, LANES), kx, w_eff


def prompt_attention(p, mode, batch, seq_len, n_heads, q_blk, k_blk, v_blk, extra=()):
    t = ATT_TILE
    nq = seq_len // t
    n_hp = n_heads // 2
    pairs = ATT_HEAD_PAIRS
    ng = n_hp // pairs
    wide = pairs * LANES
    win = (WIN_MAX // t) if mode == "dil" else nq
    steps = [(qi, ki) for qi in range(nq) for ki in range(max(0, qi - win), qi + 1)]
    qi_tab = jnp.asarray([s[0] for s in steps], jnp.int32)
    ki_tab = jnp.asarray([s[1] for s in steps], jnp.int32)
    assert t // POS_SPLIT <= 256 and X_BLOCK + seq_len // MOBA_BLOCK <= HEAD_DIM
    assert q_blk % pairs == 0 and k_blk % pairs == 0 and v_blk % pairs == 0 and n_hp % pairs == 0

    def qmap(g, s, qt, kt, wt):
        return ((g // ng) * nq + qt[s], q_blk // pairs + g % ng)

    def kmap(g, s, qt, kt, wt):
        return ((g // ng) * nq + kt[s], k_blk // pairs + g % ng)

    def vmap_(g, s, qt, kt, wt):
        return ((g // ng) * nq + kt[s], v_blk // pairs + g % ng)

    def omap(g, s, qt, kt, wt):
        return ((g // ng) * nq + qt[s], g % ng)

    if mode == "fox":
        kx, = extra
        qx = np.zeros((n_heads, LANES), np.float32)
        for hg in range(n_heads):
            qx[hg, (1 - hg % 2) * HEAD_DIM + X_FORGET:(1 - hg % 2) * HEAD_DIM + X_FORGET + N_PIECES] = 1.0
        qx = qx.reshape(n_hp, 2, LANES)
        w_eff = np.zeros((n_heads,), np.float32)
        kx_spec = pl.BlockSpec((pairs, t, LANES), lambda g, s, qt, kt, wt: (g % ng, (g // ng) * nq + kt[s], 0))
    else:
        qx, kx, w_eff = _alibi_extras(n_heads, t)
        kx = jnp.asarray(kx)
        kx_spec = pl.BlockSpec((t, LANES), lambda g, s, qt, kt, wt: (0, 0))
    in_specs = [pl.BlockSpec((t, wide), qmap), pl.BlockSpec((t, wide), kmap), pl.BlockSpec((t, wide), vmap_),
                pl.BlockSpec((pairs, 2, LANES), lambda g, s, qt, kt, wt: (g % ng, 0, 0)), kx_spec]
    args = [p, p, p, jnp.asarray(qx), kx]
    if mode == "moba":
        km, = extra
        in_specs.append(pl.BlockSpec((1, km.shape[1], wide), lambda g, s, qt, kt, wt: (g // ng, 0, g % ng)))
        args.append(km)
    elif mode == "dil":
        d = (np.arange(win + 1)[:, None, None] * t + np.arange(t)[None, :, None] - np.arange(t)[None, None, :])
        lb = jnp.asarray(_log2_multiplicity(d))
        in_specs.append(pl.BlockSpec(lb.shape, lambda g, s, qt, kt, wt: (0, 0, 0)))
        args.append(lb)
    return pl.pallas_call(
        functools.partial(_flash_body, mode=mode, n_hp=n_hp, win=win),
        out_shape=jax.ShapeDtypeStruct((batch * seq_len, n_heads * HEAD_DIM), BF16),
        grid_spec=pltpu.PrefetchScalarGridSpec(
            num_scalar_prefetch=3,
            grid=(batch * ng, len(steps)),
            in_specs=in_specs,
            out_specs=pl.BlockSpec((t, wide), omap),
            scratch_shapes=[pltpu.VMEM((2 * pairs, t, LANES), BF16), pltpu.VMEM((2 * pairs, t, LANES), F32),
                            pltpu.VMEM((2 * pairs, t, LANES), F32)]),
        compiler_params=_cparams("parallel", "arbitrary"),
        name="prompt_attention_" + mode,
    )(qi_tab, ki_tab, jnp.asarray(w_eff), *args)


def _sample_body(pt_ref, q_ref, kn_ref, vn_ref, *rest, mode, n_heads, n_steps, group, page, pages_per_block, paged):
    n_pool = group if paged else 1
    kp, vp, rest = rest[:n_pool], rest[n_pool:2 * n_pool], rest[2 * n_pool:]
    if mode == "moba":
        bias_ref, biasn_ref, o_ref, qbd_sc, kn_sc, vn_sc, acc_sc, m_sc, l_sc, sel_sc = rest
    elif mode == "fox":
        lfp, rest = rest[:group], rest[group:]
        lfn_ref, o_ref, qbd_sc, kn_sc, vn_sc, acc_sc, m_sc, l_sc, ncb_sc, run_sc = rest
    else:
        bias_ref, biasn_ref, mult_ref, multn_ref, o_ref, qbd_sc, kn_sc, vn_sc, acc_sc, m_sc, l_sc = rest
    s = pl.program_id(1)
    nq = q_ref.shape[0]
    rows, width = acc_sc.shape
    lane = lax.broadcasted_iota(jnp.int32, (rows, LANES), 1)
    row_q = lax.broadcasted_iota(jnp.int32, (rows, LANES), 0) % nq
    col_head = lax.broadcasted_iota(jnp.int32, (nq, width), 1) // HEAD_DIM

    def attend(scores, pv, bias, ok, mult):
        logit = scores + bias
        if ok is not None:
            logit = jnp.where(ok, logit, NEG)
        m_old = m_sc[...]
        m_new = jnp.maximum(m_old, jnp.max(logit, axis=-1, keepdims=True))
        alpha = jnp.exp(m_old - m_new)
        p = jnp.exp(logit - _lane_tile(m_new, logit.shape[1] // LANES))
        if mult is not None:
            p = p * mult
        l_sc[...] = alpha * l_sc[...] + jnp.sum(p, axis=-1, keepdims=True)
        acc_sc[...] = _lane_tile(alpha, width // LANES) * acc_sc[...] + pv(p.astype(BF16))
        m_sc[...] = m_new

    @pl.when(s == 0)
    def _():
        acc_sc[...] = jnp.zeros_like(acc_sc)
        l_sc[...] = jnp.zeros_like(l_sc)
        m_sc[...] = jnp.full_like(m_sc, NEG)
        kn_sc[...] = jnp.zeros_like(kn_sc)
        vn_sc[...] = jnp.zeros_like(vn_sc)
        kn_sc[0:nq, :] = kn_ref[...].astype(BF16)
        vn_sc[0:nq, :] = vn_ref[...].astype(BF16)
        q = q_ref[...]
        for h in range(n_heads):
            qbd_sc[h * nq:(h + 1) * nq, :] = jnp.where(col_head == h, q * SCALE, 0.0).astype(BF16)
        causal = lane <= row_q
        scores = lax.dot_general(qbd_sc[...], kn_sc[...], (((1,), (1,)), ((), ())), preferred_element_type=F32)
        pv_new = lambda p: jnp.dot(p, vn_sc[...], preferred_element_type=F32)
        if mode == "moba":
            nb = group // pages_per_block
            blk_lane = lax.broadcasted_iota(jnp.int32, (width, LANES), 1)
            means = jnp.zeros((width, LANES), F32)
            for blk in range(nb):
                total = sum(kp[blk * pages_per_block + g][0] for g in range(pages_per_block))
                mean = jnp.sum(total.reshape(width, page), axis=-1, keepdims=True) * (1.0 / MOBA_BLOCK)
                means = jnp.where(blk_lane == blk, mean, means)
            q_bd = jnp.concatenate([jnp.where(col_head == h, q, 0.0) for h in range(n_heads)], axis=0)
            gate = jnp.dot(q_bd, means, preferred_element_type=F32, precision=HIGHEST)
            sel_sc[...] = _top_blocks(gate, float(nb), float(nb), LANES)
            attend(scores, pv_new, biasn_ref[...], causal, None)
        elif mode == "fox":
            lfn = lfn_ref[...]
            tri = (lax.broadcasted_iota(jnp.int32, (nq, nq), 1)
                   <= lax.broadcasted_iota(jnp.int32, (nq, nq), 0)).astype(F32)
            newcum = jnp.dot(tri, lfn, preferred_element_type=F32, precision=HIGHEST)
            eye = lane[:nq] == row_q[:nq]
            ncb, nrow = [], []
            for h in range(n_heads):
                col = jnp.broadcast_to(newcum[:, h:h + 1], (nq, LANES))
                ncb.append(col)
                nrow.append(jnp.broadcast_to(jnp.sum(jnp.where(eye, col, 0.0), axis=0, keepdims=True), (nq, LANES)))
            ncb = jnp.concatenate(ncb, axis=0)
            ncb_sc[...] = ncb
            run_sc[...] = jnp.zeros_like(run_sc)
            attend(scores, pv_new, ncb - jnp.concatenate(nrow, axis=0), causal, None)
        else:
            multn = multn_ref[...]
            attend(scores, pv_new, biasn_ref[...], multn > 0.0, multn)

    if paged:
        k_t = jnp.concatenate([kp[g][0].reshape(width, page) for g in range(group)], axis=1).astype(BF16)
        v_t = jnp.concatenate([vp[g][0].reshape(width, page) for g in range(group)], axis=1).astype(BF16)
    else:
        k_t = kp[0][0].reshape(width, group * page).astype(BF16)
        v_t = vp[0][0].reshape(width, group * page).astype(BF16)
    scores = jnp.dot(qbd_sc[...], k_t, preferred_element_type=F32)
    pv = lambda p: lax.dot_general(p, v_t, (((1,), (1,)), ((), ())), preferred_element_type=F32)
    if mode == "moba":
        hidden = []
        for g in range(group):
            blk = (s * group + g) // pages_per_block
            vis = jnp.max(jnp.where(lane == blk, sel_sc[...], 0.0), axis=-1, keepdims=True)
            hidden.append(jnp.broadcast_to(jnp.where(vis > 0.5, 0.0, NEG), (rows, page)))
        attend(scores, pv, bias_ref[0] + jnp.concatenate(hidden, axis=1), None, None)
    elif mode == "fox":
        lft = jnp.concatenate([lfp[g][0] for g in range(group)], axis=0)
        upper = (lax.broadcasted_iota(jnp.int32, (page, page), 0)
                 > lax.broadcasted_iota(jnp.int32, (page, page), 1)).astype(F32)
        inside = jnp.dot(lft, upper, preferred_element_type=F32, precision=HIGHEST)
        totals = jnp.sum(lft, axis=-1, keepdims=True)
        run = run_sc[...]
        pieces = []
        for g in range(group):
            after = inside[g * n_heads:(g + 1) * n_heads] + run
            pieces.append(jnp.concatenate(
                [jnp.broadcast_to(after[h:h + 1, :], (nq, page)) for h in range(n_heads)], axis=0))
            run = run + totals[g * n_heads:(g + 1) * n_heads]
        run_sc[...] = run
        attend(scores, pv, _lane_tile(ncb_sc[...], group) + jnp.concatenate(pieces, axis=1), None, None)
    else:
        mult = mult_ref[0]
        attend(scores, pv, bias_ref[0], mult > 0.0, mult)

    @pl.when(s == n_steps - 1)
    def _():
        out = acc_sc[...] / _lane_tile(l_sc[...], width // LANES)
        o_ref[...] = sum(jnp.where(col_head == h, out[h * nq:(h + 1) * nq, :], 0.0) for h in range(n_heads))


def _multiplicity(d):
    m = np.zeros(d.shape, np.float32)
    for window, dil in DILATED_BRANCHES:
        m += ((d >= 0) & (d <= window) & (d % dil == 0)).astype(np.float32)
    return m


def _sample_tables(past_len, nq, n_heads, n_steps, span):
    d = (past_len + np.arange(nq)[None, :, None]) - (np.arange(n_steps)[:, None, None] * span
                                                     + np.arange(span)[None, None, :])
    dn = np.arange(nq)[:, None] - np.arange(LANES)[None, :]
    slopes = np.repeat(np.asarray(_alibi_slopes(n_heads), np.float64), nq)[None, :, None]
    tile = lambda a: np.tile(a, (1, n_heads, 1))
    d, dn = tile(d), tile(dn[None])
    return ((-slopes * d).astype(np.float32), (-slopes * dn)[0].astype(np.float32),
            _multiplicity(d), _multiplicity(dn)[0])


def sample_attention(p, mode, row0, db, nq, n_heads, q_blk, k_blk, v_blk, k_pool_t, v_pool_t, page_table, extra=()):
    width = n_heads * HEAD_DIM
    rows = n_heads * nq
    reverse = mode == "fox"
    paged = page_table is not None
    page = k_pool_t.shape[3] if paged else LANES
    assert page == LANES
    n_pages = page_table.shape[1] if paged else k_pool_t.shape[3] // page
    group = min(n_pages, SAMPLE_PAGED_GROUP if paged else SAMPLE_SPAN_GROUP)
    pt = page_table.reshape(-1) if paged else jnp.zeros((1,), jnp.int32)
    n_steps = n_pages // group
    span = group * page
    rb0 = row0 // nq

    def pg(s, g):
        j = s * group + g
        return (n_pages - 1 - j) if reverse else j

    def pool_spec(g):
        return pl.BlockSpec((1, n_heads, HEAD_DIM, page), lambda b, s, t: (t[b * n_pages + pg(s, g)], 0, 0, 0))

    if paged:
        pool_specs = [pool_spec(g) for g in range(group)]
        pools = lambda a: [a] * group
    else:
        pool_specs = [pl.BlockSpec((1, n_heads, HEAD_DIM, span), lambda b, s, t: (b, 0, 0, s))]
        pools = lambda a: [a]
    bias, biasn, mult, multn = [jnp.asarray(a) for a in _sample_tables(n_pages * page, nq, n_heads, n_steps, span)]
    table_spec = pl.BlockSpec((1, rows, span), lambda b, s, t: (s, 0, 0))
    new_spec = pl.BlockSpec((rows, LANES), lambda b, s, t: (0, 0))
    in_specs = [pl.BlockSpec((nq, width), lambda b, s, t: (rb0 + b, q_blk)),
                pl.BlockSpec((nq, width), lambda b, s, t: (rb0 + b, k_blk)),
                pl.BlockSpec((nq, width), lambda b, s, t: (rb0 + b, v_blk))]
    in_specs += pool_specs * 2
    args = [p, p, p] + pools(k_pool_t) + pools(v_pool_t)
    scratch = [pltpu.VMEM((rows, width), BF16), pltpu.VMEM((LANES, width), BF16), pltpu.VMEM((LANES, width), BF16),
               pltpu.VMEM((rows, width), F32), pltpu.VMEM((rows, LANES), F32), pltpu.VMEM((rows, LANES), F32)]
    if mode == "moba":
        assert n_steps == 1
        in_specs += [table_spec, new_spec]
        args += [bias, biasn]
        scratch.append(pltpu.VMEM((rows, LANES), F32))
    elif mode == "fox":
        lf_pool_t, logf = extra
        in_specs += [pl.BlockSpec((1, n_heads, page),
                                  functools.partial(lambda b, s, t, g: (t[b * n_pages + pg(s, g)], 0, 0), g=g))
                     for g in range(group)]
        in_specs.append(pl.BlockSpec((nq, LANES), lambda b, s, t: (rb0 + b, 0)))
        args += [lf_pool_t] * group + [logf]
        scratch += [pltpu.VMEM((rows, LANES), F32), pltpu.VMEM((n_heads, LANES), F32)]
    else:
        in_specs += [table_spec, new_spec, table_spec, new_spec]
        args += [bias, biasn, mult, multn]
    return pl.pallas_call(
        functools.partial(_sample_body, mode=mode, n_heads=n_heads, n_steps=n_steps, group=group, page=page,
                          pages_per_block=MOBA_BLOCK // page, paged=paged),
        out_shape=jax.ShapeDtypeStruct((db * nq, width), F32),
        grid_spec=pltpu.PrefetchScalarGridSpec(
            num_scalar_prefetch=1,
            grid=(db, n_steps),
            in_specs=in_specs,
            out_specs=pl.BlockSpec((nq, width), lambda b, s, t: (b, 0)),
            scratch_shapes=scratch),
        compiler_params=_cparams("parallel", "arbitrary"),
        name="sample_attention_" + mode,
    )(pt, *args)


RANK_NONE = 127.0


def _top_rows(x, k, with_rank=False):
    vals = []
    rank = jnp.full(x.shape, RANK_NONE, F32)
    for r in range(k):
        m = jnp.max(x, axis=0, keepdims=True)
        vals.append(m)
        hit = x == m
        if with_rank:
            rank = jnp.where(hit, float(r), rank)
        x = jnp.where(hit, -jnp.inf, x)
    return (vals, rank) if with_rank else vals


def _bf16_twice(x):
    bits = lax.bitcast_convert_type(x.astype(BF16).astype(F32), jnp.uint32)
    return bits | (bits >> 16)


def _peer_scores_body(ht_ref, wq_ref, sk_ref, n1_ref, r2_ref, e1_ref, e2_ref):
    half = PEER_KEYS
    k1 = PEER_TOPK + 1
    qt = jnp.dot(wq_ref[...], ht_ref[...], preferred_element_type=F32)
    s1 = jnp.dot(sk_ref[0, 0], qt[:half], preferred_element_type=F32, precision=HIGHEST)
    s2 = jnp.dot(sk_ref[0, 1], qt[half:], preferred_element_type=F32, precision=HIGHEST)
    t1 = _top_rows(s1, k1)
    t2, rank2 = _top_rows(s2, k1, with_rank=True)
    cand = [t1[i] + t2[j] for i in range(k1) for j in range(k1) if (i + 1) * (j + 1) <= k1]
    pad = -len(cand) % SUBLANES
    cand = jnp.concatenate(cand + [jnp.full_like(t1[0], -jnp.inf)] * pad, axis=0)
    top = _top_rows(cand, k1)
    z = sum(jnp.exp(v - top[0]) for v in top[:PEER_TOPK])
    tau = 0.5 * (top[PEER_TOPK - 1] + top[PEER_TOPK])
    need = tau - s1
    n1 = jnp.zeros_like(s1)
    for j in range(k1):
        n1 = jnp.where(t2[j] >= need, float(j + 1), n1)
    n1_ref[0] = _bf16_twice(n1)
    r2_ref[0] = pltpu.bitcast(rank2.astype(BF16), jnp.uint32)
    e1_ref[0] = _bf16_twice(jnp.exp(s1 - t1[0]) / z)
    e2_ref[0] = pltpu.bitcast(jnp.exp(s2 - t2[0]).astype(BF16), jnp.uint32)


def peer_scores(ht, wq_t, subkeys):
    d, t = ht.shape
    tt = PEER_TOKEN_TILE
    nh = PEER_HEADS
    dq = wq_t.shape[0] // nh
    big = jax.ShapeDtypeStruct((nh, PEER_KEYS, t), jnp.uint32)
    small = jax.ShapeDtypeStruct((nh, PEER_KEYS // 2, t), jnp.uint32)
    bspec = pl.BlockSpec((1, PEER_KEYS, tt), lambda i, h: (h, 0, i))
    pspec = pl.BlockSpec((1, PEER_KEYS // 2, tt), lambda i, h: (h, 0, i))
    return pl.pallas_call(
        _peer_scores_body,
        out_shape=(big, small, big, small),
        grid=(t // tt, nh),
        in_specs=[pl.BlockSpec((d, tt), lambda i, h: (0, i)),
                  pl.BlockSpec((dq, d), lambda i, h: (h, 0)),
                  pl.BlockSpec((1,) + subkeys.shape[1:], lambda i, h: (h, 0, 0, 0))],
        out_specs=(bspec, pspec, bspec, pspec),
        compiler_params=_cparams("parallel", "arbitrary"),
        name="peer_scores",
    )(ht, wq_t, subkeys)


def _gelu(x):
    return 0.5 * x * (1.0 + lax.erf(x * (1.0 / math.sqrt(2.0))))


def _peer_experts_body(ht_ref, u_ref, vt_ref, n1_ref, r2_ref, e1_ref, e2_ref, x_ref, g_ref,
                       xn_ref, hn_ref, acc_sc, act_sc, w_sc):
    j = pl.program_id(1)
    te = u_ref.shape[0]
    tt = ht_ref.shape[1]
    groups = te // PEER_KEYS

    @pl.when(j == 0)
    def _():
        acc_sc[...] = jnp.zeros_like(acc_sc)

    act_sc[...] = _gelu(jnp.dot(u_ref[...], ht_ref[...], preferred_element_type=F32)).astype(BF16)
    n_tc = tt // LANES
    packed_rows = (PEER_KEYS // 2, LANES)
    for al in range(groups):
        rows = slice(al * PEER_KEYS, (al + 1) * PEER_KEYS)
        coef = [jnp.zeros((PEER_KEYS, LANES), BF16) for _ in range(n_tc)]
        for h in range(PEER_HEADS):
            n1_row = n1_ref[h, al:al + 1, :]
            e1_row = e1_ref[h, al:al + 1, :]
            for tc in range(n_tc):
                cols = slice(tc * LANES, (tc + 1) * LANES)
                n1 = pltpu.bitcast(jnp.broadcast_to(n1_row[:, cols], packed_rows), BF16)
                e1 = pltpu.bitcast(jnp.broadcast_to(e1_row[:, cols], packed_rows), BF16)
                hit = pltpu.bitcast(r2_ref[h, :, cols], BF16) < n1
                e2 = pltpu.bitcast(e2_ref[h, :, cols], BF16)
                coef[tc] = coef[tc] + jnp.where(hit, e2, jnp.zeros((), BF16)) * e1
        for tc in range(n_tc):
            cols = slice(tc * LANES, (tc + 1) * LANES)
            w_sc[rows, cols] = coef[tc] * act_sc[rows, cols]
    acc_sc[...] += jnp.dot(vt_ref[...], w_sc[...], preferred_element_type=F32)

    @pl.when(j == pl.num_programs(1) - 1)
    def _():
        xn = x_ref[...] + acc_sc[...].T
        xn_ref[...] = xn
        hn_ref[...] = _rms(xn, g_ref[...]).astype(hn_ref.dtype)


def peer_experts(x, ht, u, v_t, scores, g_next, norm_dtype):
    t, d = x.shape
    tt = PEER_TOKEN_TILE
    te = PEER_EXPERT_TILE
    n_exp = u.shape[0]
    sspec = pl.BlockSpec((PEER_HEADS, te // PEER_KEYS, tt), lambda i, j: (0, j, i))
    pspec = pl.BlockSpec((PEER_HEADS, PEER_KEYS // 2, tt), lambda i, j: (0, 0, i))
    return pl.pallas_call(
        _peer_experts_body,
        out_shape=(jax.ShapeDtypeStruct((t, d), F32), jax.ShapeDtypeStruct((t, d), norm_dtype)),
        grid=(t // tt, n_exp // te),
        in_specs=[pl.BlockSpec((d, tt), lambda i, j: (0, i)),
                  pl.BlockSpec((te, d), lambda i, j: (j, 0)),
                  pl.BlockSpec((d, te), lambda i, j: (0, j)),
                  sspec, pspec, sspec, pspec,
                  pl.BlockSpec((tt, d), lambda i, j: (i, 0)),
                  pl.BlockSpec((1, d), lambda i, j: (0, 0))],
        out_specs=(pl.BlockSpec((tt, d), lambda i, j: (i, 0)),
                   pl.BlockSpec((tt, d), lambda i, j: (i, 0))),
        scratch_shapes=[pltpu.VMEM((d, tt), F32), pltpu.VMEM((te, tt), BF16), pltpu.VMEM((te, tt), BF16)],
        compiler_params=_cparams("parallel", "arbitrary"),
        name="peer_experts",
    )(ht, u, v_t, *scores, x, g_next.reshape(1, d))


def _head_major_body(cb_ref, x_ref, o_ref):
    n_heads = o_ref.shape[2]
    o_ref[0, 0] = x_ref[...].T.reshape(n_heads, HEAD_DIM, x_ref.shape[0])


def head_major(p, col_blocks, batch, seq_len, row_start, rows, n_heads):
    width = n_heads * HEAD_DIM
    t = ROW_TILE
    assert rows % t == 0 and row_start % t == 0 and seq_len % t == 0
    return pl.pallas_call(
        _head_major_body,
        out_shape=jax.ShapeDtypeStruct((len(col_blocks), batch, n_heads, HEAD_DIM, rows), F32),
        grid_spec=pltpu.PrefetchScalarGridSpec(
            num_scalar_prefetch=1,
            grid=(len(col_blocks), batch, rows // t),
            in_specs=[pl.BlockSpec((t, width), lambda g, b, i, cb: ((b * seq_len + row_start) // t + i, cb[g]))],
            out_specs=pl.BlockSpec((1, 1, n_heads, HEAD_DIM, t), lambda g, b, i, cb: (g, b, 0, 0, i))),
        compiler_params=_cparams("parallel", "parallel", "parallel"),
        name="head_major",
    )(jnp.asarray(col_blocks, jnp.int32), p)


def _heads(a, n):
    return a.reshape(a.shape[:-1] + (n, HEAD_DIM))


def kernel(x_prompt, x_sample, cache_moba_k, cache_moba_v, cache_fox_k, cache_fox_v, cache_fox_logf,
           state_swa_k, state_swa_v, page_table, norm_gain, final_gain, w_in_even, b_forget, w_out_even,
           w_in_odd, w_out_odd, peer_w_query, peer_subkeys, peer_u, peer_v):
    b, s, d = x_prompt.shape
    db, ds, _ = x_sample.shape
    tp = b * s
    depth = norm_gain.shape[0]
    n_pool, page = cache_moba_k.shape[1:3]
    past_len = page_table.shape[1] * page
    win_buf = state_swa_k.shape[2]
    keep_p = min(WIN_MAX, s)
    assert win_buf == past_len and past_len % MOBA_BLOCK == 0 and MOBA_BLOCK % page == 0
    assert s % ATT_TILE == 0 and (tp + db * ds) % ROW_TILE == 0 and tp % ds == 0

    x = jnp.concatenate([x_prompt.reshape(tp, d), x_sample.reshape(db * ds, d)], axis=0)
    h = rmsnorm_rows(x, norm_gain[0, 0], BF16)
    new = {}
    for layer in range(depth):
        li = layer // 2
        if layer % 2 == 0:
            n_in = w_in_even.shape[2]
            n_pad = -n_in % (5 * LANES)
            w_in = jnp.pad(w_in_even[li], ((0, 0), (0, n_pad))).astype(BF16)
            p = matmul(h, w_in, (n_in + n_pad) // 5)
            logf, fox_kx = fox_gate(p, 3 * (D_A + D_B) // LANES, b_forget[li], s)
            km = moba_block_means(p, 1, b, s)
            oa_p = prompt_attention(p, "moba", b, s, N_HEADS_A, 0, D_A // LANES, 2 * D_A // LANES, (km,))
            fb = 3 * D_A // LANES
            ob_p = prompt_attention(p, "fox", b, s, N_HEADS_B, fb, fb + D_B // LANES, fb + 2 * D_B // LANES,
                                    (fox_kx,))
            pool = lambda c: jnp.transpose(c[li], (0, 2, 3, 1))
            ka_pool, va_pool = pool(cache_moba_k), pool(cache_moba_v)
            kb_pool, vb_pool = pool(cache_fox_k), pool(cache_fox_v)
            oa_s = sample_attention(p, "moba", tp, db, ds, N_HEADS_A, 0, 1, 2, ka_pool, va_pool, page_table)
            lf_pool_t = jnp.swapaxes(cache_fox_logf[li], 1, 2)
            ob_s = sample_attention(p, "fox", tp, db, ds, N_HEADS_B, 3, 4, 5, kb_pool, vb_pool, page_table,
                                    (lf_pool_t, logf))
            w_out = w_out_even[li].astype(BF16)
            pairs = [(jnp.concatenate([oa_p, oa_s.astype(BF16)], axis=0), w_out[:D_A]),
                     (jnp.concatenate([ob_p, ob_s.astype(BF16)], axis=0), w_out[D_A:])]
            assert N_HEADS_A == N_HEADS_B
            names = ("mk", "mv", "fk", "fv")
            col_blocks = (1, 2, 4, 5)
            stored = head_major(p, col_blocks, b, s, 0, s, N_HEADS_A)
            for g, name in enumerate(names):
                cols = p[tp:, col_blocks[g] * D_A:(col_blocks[g] + 1) * D_A]
                new.setdefault("p_" + name, []).append(jnp.transpose(stored[g], (0, 3, 1, 2)))
                new.setdefault("s_" + name, []).append(_heads(cols.reshape(db, ds, -1), N_HEADS_A))
            new.setdefault("p_fl", []).append(logf[:tp, :N_HEADS_B].reshape(b, s, N_HEADS_B))
            new.setdefault("s_fl", []).append(logf[tp:, :N_HEADS_B].reshape(db, ds, N_HEADS_B))
        else:
            p = matmul(h, w_in_odd[li].astype(BF16), D_C // 2)
            nb = D_C // LANES
            oc_p = prompt_attention(p, "dil", b, s, N_HEADS_C, 0, nb, 2 * nb)
            kc_pool = jnp.transpose(state_swa_k[li], (0, 2, 3, 1))
            vc_pool = jnp.transpose(state_swa_v[li], (0, 2, 3, 1))
            oc_s = sample_attention(p, "dil", tp, db, ds, N_HEADS_C, 0, 1, 2, kc_pool, vc_pool, None)
            pairs = [(jnp.concatenate([oc_p, oc_s.astype(BF16)], axis=0), w_out_odd[li].astype(BF16))]
            stored = head_major(p, (1, 2), b, s, s - keep_p, keep_p, N_HEADS_C)
            for g, name in enumerate(("sk", "sv")):
                cols = p[tp:, (g + 1) * D_C:(g + 2) * D_C]
                new.setdefault("p_" + name, []).append(jnp.transpose(stored[g], (0, 3, 1, 2)))
                new.setdefault("s_" + name, []).append(_heads(cols.reshape(db, ds, -1), N_HEADS_C))
        x, _, h_t = resid_norm(x, pairs, norm_gain[layer, 1])
        scores = peer_scores(h_t, peer_w_query[layer].T.astype(BF16), peer_subkeys[layer])
        last = layer == depth - 1
        g_next = final_gain if last else norm_gain[layer + 1, 0]
        x, h = peer_experts(x, h_t, peer_u[layer].astype(BF16), peer_v[layer].T.astype(BF16), scores,
                            g_next, F32 if last else BF16)
    y = h
    order = ["p_mk", "p_mv", "p_fk", "p_fv", "p_fl", "p_sk", "p_sv",
             "s_mk", "s_mv", "s_fk", "s_fv", "s_fl", "s_sk", "s_sv"]
    return (y[:tp].reshape(b, s, d), y[tp:].reshape(db, ds, d)) + tuple(jnp.stack(new[k]) for k in order)
```

```python
import functools
import math

import numpy as np
import jax
import jax.numpy as jnp
from jax import lax
from jax.experimental import pallas as pl
from jax.experimental.pallas import tpu as pltpu

F32 = jnp.float32
BF16 = jnp.bfloat16
HIGHEST = lax.Precision.HIGHEST

LANES = 128
SUBLANES = 8
VMEM_LIMIT_BYTES = 56 * 1024 * 1024

HEAD_DIM = 64
N_HEADS_A = 8
N_HEADS_B = 8
N_HEADS_C = 16
D_A = N_HEADS_A * HEAD_DIM
D_B = N_HEADS_B * HEAD_DIM
D_C = N_HEADS_C * HEAD_DIM
MOBA_BLOCK = 256
MOBA_TOPK = 3
DILATED_BRANCHES = ((128, 1), (512, 4), (2048, 16))
WIN_MAX = 2048
PEER_HEADS = 8
PEER_KEYS = 128
PEER_TOPK = 16
RMS_EPS = 1e-6
NEG = -1e30
SCALE = HEAD_DIM ** -0.5
LOG2E = 1.0 / math.log(2.0)

ATT_TILE = 512
ATT_HEAD_PAIRS = 4
GATE_TILE = 256
ROW_TILE = 1024
PEER_TOKEN_TILE = 512
PEER_EXPERT_TILE = 1024
SAMPLE_PAGED_GROUP = 16
SAMPLE_SPAN_GROUP = 8

X_POS_HI = 0
X_POS_LO = 3
X_FORGET = 0
X_BLOCK = 8
POS_SPLIT = 16
N_PIECES = 3


def _cparams(*sem):
    return pltpu.CompilerParams(dimension_semantics=sem, vmem_limit_bytes=VMEM_LIMIT_BYTES)


def _alibi_slopes(n):
    return [2.0 ** (-8.0 * (i + 1) / n) for i in range(n)]


def _bf16_pieces_np(x):
    x = np.asarray(x, np.float32)
    out = []
    for _ in range(N_PIECES):
        p = np.asarray(np.asarray(x, dtype=BF16), np.float32)
        out.append(p)
        x = x - p
    return out


def _rms(x, g):
    return x * lax.rsqrt(jnp.mean(x * x, axis=-1, keepdims=True) + RMS_EPS) * g


def _lane_tile(x, n):
    return x if n == 1 else jnp.concatenate([x] * n, axis=1)


def _rms_body(x_ref, g_ref, h_ref):
    h_ref[...] = _rms(x_ref[...], g_ref[...]).astype(h_ref.dtype)


def rmsnorm_rows(x, g, out_dtype):
    t, d = x.shape
    return pl.pallas_call(
        _rms_body,
        out_shape=jax.ShapeDtypeStruct((t, d), out_dtype),
        grid=(t // ROW_TILE,),
        in_specs=[pl.BlockSpec((ROW_TILE, d), lambda i: (i, 0)),
                  pl.BlockSpec((1, d), lambda i: (0, 0))],
        out_specs=pl.BlockSpec((ROW_TILE, d), lambda i: (i, 0)),
        compiler_params=_cparams("parallel"),
        name="rmsnorm_rows",
    )(x, g.reshape(1, d))


def _mm_body(h_ref, w_ref, o_ref):
    o_ref[...] = jnp.dot(h_ref[...], w_ref[...], preferred_element_type=F32)


def matmul(h, w, tn):
    t, k = h.shape
    n = w.shape[1]
    return pl.pallas_call(
        _mm_body,
        out_shape=jax.ShapeDtypeStruct((t, n), F32),
        grid=(t // ROW_TILE, n // tn),
        in_specs=[pl.BlockSpec((ROW_TILE, k), lambda i, j: (i, 0)),
                  pl.BlockSpec((k, tn), lambda i, j: (0, j))],
        out_specs=pl.BlockSpec((ROW_TILE, tn), lambda i, j: (i, j)),
        compiler_params=_cparams("parallel", "arbitrary"),
        name="matmul",
    )(h, w)


def _resid_norm_body(*refs, n_pairs):
    x_ref = refs[0]
    g_ref = refs[1 + 2 * n_pairs]
    xn_ref, h_ref, ht_ref = refs[2 + 2 * n_pairs:]
    acc = x_ref[...]
    for p in range(n_pairs):
        acc = acc + jnp.dot(refs[1 + 2 * p][...], refs[2 + 2 * p][...], preferred_element_type=F32)
    xn_ref[...] = acc
    h = _rms(acc, g_ref[...])
    h_ref[...] = h.astype(h_ref.dtype)
    ht_ref[...] = h.T.astype(ht_ref.dtype)


def resid_norm(x, pairs, g):
    t, d = x.shape
    args = [x]
    in_specs = [pl.BlockSpec((ROW_TILE, d), lambda i: (i, 0))]
    for a, w in pairs:
        args += [a, w]
        in_specs += [pl.BlockSpec((ROW_TILE, a.shape[1]), lambda i: (i, 0)),
                     pl.BlockSpec(w.shape, lambda i: (0, 0))]
    args.append(g.reshape(1, d))
    in_specs.append(pl.BlockSpec((1, d), lambda i: (0, 0)))
    return pl.pallas_call(
        functools.partial(_resid_norm_body, n_pairs=len(pairs)),
        out_shape=(jax.ShapeDtypeStruct((t, d), F32),
                   jax.ShapeDtypeStruct((t, d), BF16),
                   jax.ShapeDtypeStruct((d, t), BF16)),
        grid=(t // ROW_TILE,),
        in_specs=in_specs,
        out_specs=(pl.BlockSpec((ROW_TILE, d), lambda i: (i, 0)),
                   pl.BlockSpec((ROW_TILE, d), lambda i: (i, 0)),
                   pl.BlockSpec((d, ROW_TILE), lambda i: (0, i))),
        compiler_params=_cparams("parallel"),
        name="resid_norm",
    )(*args)


def _log_sigmoid(x):
    return -(jnp.maximum(-x, 0.0) + jnp.log1p(jnp.exp(-jnp.abs(x))))


def _fox_gate_body(p_ref, b_ref, place_ref, lf_ref, kx_ref, carry_sc, *, tiles_per_seq):
    i = pl.program_id(0)
    t = p_ref.shape[0]

    @pl.when(i % tiles_per_seq == 0)
    def _():
        carry_sc[...] = jnp.zeros_like(carry_sc)

    lf = _log_sigmoid(p_ref[...] + b_ref[...])
    lf_ref[...] = lf
    row = lax.broadcasted_iota(jnp.int32, (t, t), 0)
    col = lax.broadcasted_iota(jnp.int32, (t, t), 1)
    tri = (col <= row).astype(F32)
    c = jnp.dot(tri, lf, preferred_element_type=F32, precision=HIGHEST) + carry_sc[0:1, :]
    carry_sc[...] = jnp.broadcast_to(c[t - 1:t, :], carry_sc.shape)
    rest = c * LOG2E
    pieces = []
    for _ in range(N_PIECES):
        piece = rest.astype(BF16)
        pieces.append(piece)
        rest = rest - piece.astype(F32)
    for hp in range(kx_ref.shape[0]):
        moved = sum(jnp.dot(pieces[k], place_ref[hp, k], preferred_element_type=F32) for k in range(N_PIECES))
        kx_ref[hp] = (-moved).astype(BF16)


def fox_gate(p, col_block, b_forget, seq_len):
    t = p.shape[0]
    n_heads = b_forget.shape[0]
    n_hp = n_heads // 2
    b = jnp.zeros((1, LANES), F32).at[0, :n_heads].set(b_forget)
    place = np.zeros((n_hp, N_PIECES, LANES, LANES), np.float32)
    for hp in range(n_hp):
        for k in range(N_PIECES):
            place[hp, k, 2 * hp, HEAD_DIM + X_FORGET + k] = 1.0
            place[hp, k, 2 * hp + 1, X_FORGET + k] = 1.0
    return pl.pallas_call(
        functools.partial(_fox_gate_body, tiles_per_seq=seq_len // GATE_TILE),
        out_shape=(jax.ShapeDtypeStruct((t, LANES), F32),
                   jax.ShapeDtypeStruct((n_hp, t, LANES), BF16)),
        grid=(t // GATE_TILE,),
        in_specs=[pl.BlockSpec((GATE_TILE, LANES), lambda i: (i, col_block)),
                  pl.BlockSpec((1, LANES), lambda i: (0, 0)),
                  pl.BlockSpec(place.shape, lambda i: (0, 0, 0, 0))],
        out_specs=(pl.BlockSpec((GATE_TILE, LANES), lambda i: (i, 0)),
                   pl.BlockSpec((n_hp, GATE_TILE, LANES), lambda i: (0, i, 0))),
        scratch_shapes=[pltpu.VMEM((SUBLANES, LANES), F32)],
        compiler_params=_cparams("arbitrary"),
        name="fox_gate",
    )(p, b, jnp.asarray(place, BF16))


def _kmean_body(k_ref, o_ref):
    j = pl.program_id(1)
    o_ref[0, pl.ds(j, 1), :] = jnp.sum(k_ref[...], axis=0, keepdims=True) * (1.0 / MOBA_BLOCK)


def moba_block_means(p, k_col_block, batch, seq_len):
    nb = seq_len // MOBA_BLOCK
    return pl.pallas_call(
        _kmean_body,
        out_shape=jax.ShapeDtypeStruct((batch, nb, D_A), F32),
        grid=(batch, nb),
        in_specs=[pl.BlockSpec((MOBA_BLOCK, D_A), lambda b, j: (b * nb + j, k_col_block))],
        out_specs=pl.BlockSpec((1, nb, D_A), lambda b, j: (b, 0, 0)),
        compiler_params=_cparams("parallel", "arbitrary"),
        name="moba_block_means",
    )(p)


def _top_blocks(gate, n_valid, own, n_blocks, axis=1):
    jidx = lax.broadcasted_iota(jnp.int32, gate.shape, axis).astype(F32)
    gate = jnp.where(jidx < n_valid, gate, NEG)
    sel = (jidx == own).astype(F32)
    for _ in range(MOBA_TOPK):
        mx = jnp.max(gate, axis=axis, keepdims=True)
        am = jnp.min(jnp.where(gate == mx, jidx, float(n_blocks)), axis=axis, keepdims=True)
        hit = jidx == am
        sel = jnp.where(jnp.logical_and(hit, am < n_valid), 1.0, sel)
        gate = jnp.where(hit, -jnp.inf, gate)
    return sel


def _flash_body(qi_tab, ki_tab, w_tab, q_ref, k_ref, v_ref, qx_ref, kx_ref, *rest, mode, n_hp, win):
    if mode == "moba":
        km_ref, o_ref, qm_sc, acc_sc, m_sc = rest
    elif mode == "fox":
        o_ref, qm_sc, acc_sc, m_sc = rest
    else:
        lb_ref, o_ref, qm_sc, acc_sc, m_sc = rest
    t = q_ref.shape[0]
    pairs = q_ref.shape[1] // LANES
    hp0 = (pl.program_id(0) % (n_hp // pairs)) * pairs
    s = pl.program_id(1)
    qi = qi_tab[s]
    ki = ki_tab[s]
    delta = qi - ki
    lane = lax.broadcasted_iota(jnp.int32, (t, LANES), 1)
    lo = lane < HEAD_DIM
    own_half = (lo, jnp.logical_not(lo))
    pair_lanes = [slice(pp * LANES, (pp + 1) * LANES) for pp in range(pairs)]

    @pl.when(ki == jnp.maximum(qi - win, 0))
    def _():
        acc_sc[...] = jnp.zeros_like(acc_sc)
        m_sc[...] = jnp.full_like(m_sc, NEG)
        for pp in range(pairs):
            q = q_ref[:, pair_lanes[pp]]
            qs = q * (SCALE * LOG2E)
            for h in range(2):
                qm = jnp.where(own_half[h], qs, qx_ref[pp, h:h + 1, :])
                if mode == "moba":
                    km = km_ref[0, :, pair_lanes[pp]]
                    nb = km.shape[0]
                    gate = lax.dot_general(km, jnp.where(own_half[h], q, 0.0), (((1,), (1,)), ((), ())),
                                           preferred_element_type=F32, precision=HIGHEST)
                    own_blk = (qi * (t // MOBA_BLOCK)
                               + lax.broadcasted_iota(jnp.int32, (1, t), 1) // MOBA_BLOCK).astype(F32)
                    hidden = (_top_blocks(gate, own_blk, own_blk, nb, axis=0) - 1.0) * (-NEG)
                    base = (1 - h) * HEAD_DIM + X_BLOCK
                    place = (lax.broadcasted_iota(jnp.int32, (nb, LANES), 1)
                             == lax.broadcasted_iota(jnp.int32, (nb, LANES), 0) + base).astype(BF16)
                    qm = qm + lax.dot_general(hidden.astype(BF16), place, (((0,), (0,)), ((), ())),
                                              preferred_element_type=F32)
                qm_sc[pp * 2 + h] = qm.astype(BF16)

    def step(causal):
        if mode != "fox":
            kx = kx_ref[...]
        if mode == "moba":
            key_blk = ki * (t // MOBA_BLOCK) + lax.broadcasted_iota(jnp.int32, (t, LANES), 0) // MOBA_BLOCK
            kx = jnp.where((lane % HEAD_DIM) == key_blk + X_BLOCK, 1.0, kx)
        if causal:
            keep = (lax.broadcasted_iota(jnp.int32, (t, t), 1) <= lax.broadcasted_iota(jnp.int32, (t, t), 0))
        for pp in range(pairs):
            k2 = k_ref[:, pair_lanes[pp]]
            v2 = v_ref[:, pair_lanes[pp]]
            if mode == "fox":
                kx = kx_ref[pp].astype(F32)
            for h in range(2):
                hs = pp * 2 + h
                k_aug = jnp.where(own_half[h], k2, kx).astype(BF16)
                sc = lax.dot_general(qm_sc[hs], k_aug, (((1,), (1,)), ((), ())), preferred_element_type=F32)
                if mode == "dil":
                    sc = sc + lb_ref[delta]
                if causal:
                    sc = jnp.where(keep, sc, NEG)
                off = 0.0 if mode == "fox" else w_tab[(hp0 + pp) * 2 + h] * (ki * t).astype(F32)
                m_old = m_sc[hs]
                m_new = jnp.maximum(m_old, jnp.max(sc, axis=-1, keepdims=True) + off)
                alpha = jnp.exp2(m_old - m_new)
                p = jnp.exp2(sc - _lane_tile(m_new - off, t // LANES))
                v_aug = jnp.where(own_half[h], v2, 1.0).astype(BF16)
                acc_sc[hs] = alpha * acc_sc[hs] + jnp.dot(p.astype(BF16), v_aug, preferred_element_type=F32)
                m_sc[hs] = m_new

    if mode == "dil":
        step(False)
    else:
        pl.when(delta == 0)(lambda: step(True))
        pl.when(delta != 0)(lambda: step(False))

    @pl.when(ki == qi)
    def _():
        for pp in range(pairs):
            a0 = acc_sc[pp * 2]
            a1 = acc_sc[pp * 2 + 1]
            o_ref[:, pair_lanes[pp]] = jnp.where(lo, a0 / a0[:, HEAD_DIM:HEAD_DIM + 1],
                                                 a1 / a1[:, 0:1]).astype(o_ref.dtype)


def _log2_multiplicity(d):
    m = np.zeros(d.shape, np.float64)
    for window, dil in DILATED_BRANCHES:
        m += ((d >= 0) & (d <= window) & (d % dil == 0))
    return np.where(m > 0, np.log2(np.maximum(m, 1.0)), NEG).astype(np.float32)


def _alibi_extras(n_heads, t):
    qx = np.zeros((n_heads, LANES), np.float32)
    w_eff = np.zeros((n_heads,), np.float32)
    for hg, slope in enumerate(_alibi_slopes(n_heads)):
        base = (1 - hg % 2) * HEAD_DIM
        for k, piece in enumerate(_bf16_pieces_np(slope * LOG2E)):
            qx[hg, base + X_POS_HI + k] = POS_SPLIT * piece
            qx[hg, base + X_POS_LO + k] = piece
            w_eff[hg] += piece
    col = np.arange(t)
    kx = np.zeros((t, LANES), np.float32)
    for base in (0, HEAD_DIM):
        for k in range(N_PIECES):
            kx[:, base + X_POS_HI + k] = col // POS_SPLIT
            kx[:, base + X_POS_LO + k] = col % POS_SPLIT
    return qx.reshape(n_heads // 2, 2, LANES), kx, w_eff


def prompt_attention(p, mode, batch, seq_len, n_heads, q_blk, k_blk, v_blk, extra=()):
    t = ATT_TILE
    nq = seq_len // t
    n_hp = n_heads // 2
    pairs = ATT_HEAD_PAIRS
    ng = n_hp // pairs
    wide = pairs * LANES
    win = (WIN_MAX // t) if mode == "dil" else nq
    steps = [(qi, ki) for qi in range(nq) for ki in range(max(0, qi - win), qi + 1)]
    qi_tab = jnp.asarray([s[0] for s in steps], jnp.int32)
    ki_tab = jnp.asarray([s[1] for s in steps], jnp.int32)
    assert t // POS_SPLIT <= 256 and X_BLOCK + seq_len // MOBA_BLOCK <= HEAD_DIM
    assert q_blk % pairs == 0 and k_blk % pairs == 0 and v_blk % pairs == 0 and n_hp % pairs == 0

    def qmap(g, s, qt, kt, wt):
        return ((g // ng) * nq + qt[s], q_blk // pairs + g % ng)

    def kmap(g, s, qt, kt, wt):
        return ((g // ng) * nq + kt[s], k_blk // pairs + g % ng)

    def vmap_(g, s, qt, kt, wt):
        return ((g // ng) * nq + kt[s], v_blk // pairs + g % ng)

    def omap(g, s, qt, kt, wt):
        return ((g // ng) * nq + qt[s], g % ng)

    if mode == "fox":
        kx, = extra
        qx = np.zeros((n_heads, LANES), np.float32)
        for hg in range(n_heads):
            qx[hg, (1 - hg % 2) * HEAD_DIM + X_FORGET:(1 - hg % 2) * HEAD_DIM + X_FORGET + N_PIECES] = 1.0
        qx = qx.reshape(n_hp, 2, LANES)
        w_eff = np.zeros((n_heads,), np.float32)
        kx_spec = pl.BlockSpec((pairs, t, LANES), lambda g, s, qt, kt, wt: (g % ng, (g // ng) * nq + kt[s], 0))
    else:
        qx, kx, w_eff = _alibi_extras(n_heads, t)
        kx = jnp.asarray(kx)
        kx_spec = pl.BlockSpec((t, LANES), lambda g, s, qt, kt, wt: (0, 0))
    in_specs = [pl.BlockSpec((t, wide), qmap), pl.BlockSpec((t, wide), kmap), pl.BlockSpec((t, wide), vmap_),
                pl.BlockSpec((pairs, 2, LANES), lambda g, s, qt, kt, wt: (g % ng, 0, 0)), kx_spec]
    args = [p, p, p, jnp.asarray(qx), kx]
    if mode == "moba":
        km, = extra
        in_specs.append(pl.BlockSpec((1, km.shape[1], wide), lambda g, s, qt, kt, wt: (g // ng, 0, g % ng)))
        args.append(km)
    elif mode == "dil":
        d = (np.arange(win + 1)[:, None, None] * t + np.arange(t)[None, :, None] - np.arange(t)[None, None, :])
        lb = jnp.asarray(_log2_multiplicity(d))
        in_specs.append(pl.BlockSpec(lb.shape, lambda g, s, qt, kt, wt: (0, 0, 0)))
        args.append(lb)
    return pl.pallas_call(
        functools.partial(_flash_body, mode=mode, n_hp=n_hp, win=win),
        out_shape=jax.ShapeDtypeStruct((batch * seq_len, n_heads * HEAD_DIM), BF16),
        grid_spec=pltpu.PrefetchScalarGridSpec(
            num_scalar_prefetch=3,
            grid=(batch * ng, len(steps)),
            in_specs=in_specs,
            out_specs=pl.BlockSpec((t, wide), omap),
            scratch_shapes=[pltpu.VMEM((2 * pairs, t, LANES), BF16), pltpu.VMEM((2 * pairs, t, LANES), F32),
                            pltpu.VMEM((2 * pairs, t, LANES), F32)]),
        compiler_params=_cparams("parallel", "arbitrary"),
        name="prompt_attention_" + mode,
    )(qi_tab, ki_tab, jnp.asarray(w_eff), *args)


def _sample_body(pt_ref, q_ref, kn_ref, vn_ref, *rest, mode, n_heads, n_steps, group, page, pages_per_block, paged):
    n_pool = group if paged else 1
    kp, vp, rest = rest[:n_pool], rest[n_pool:2 * n_pool], rest[2 * n_pool:]
    if mode == "moba":
        bias_ref, biasn_ref, o_ref, qbd_sc, kn_sc, vn_sc, acc_sc, m_sc, l_sc, sel_sc = rest
    elif mode == "fox":
        lfp, rest = rest[:group], rest[group:]
        lfn_ref, o_ref, qbd_sc, kn_sc, vn_sc, acc_sc, m_sc, l_sc, ncb_sc, run_sc = rest
    else:
        bias_ref, biasn_ref, mult_ref, multn_ref, o_ref, qbd_sc, kn_sc, vn_sc, acc_sc, m_sc, l_sc = rest
    s = pl.program_id(1)
    nq = q_ref.shape[0]
    rows, width = acc_sc.shape
    lane = lax.broadcasted_iota(jnp.int32, (rows, LANES), 1)
    row_q = lax.broadcasted_iota(jnp.int32, (rows, LANES), 0) % nq
    col_head = lax.broadcasted_iota(jnp.int32, (nq, width), 1) // HEAD_DIM

    def attend(scores, pv, bias, ok, mult):
        logit = scores + bias
        if ok is not None:
            logit = jnp.where(ok, logit, NEG)
        m_old = m_sc[...]
        m_new = jnp.maximum(m_old, jnp.max(logit, axis=-1, keepdims=True))
        alpha = jnp.exp(m_old - m_new)
        p = jnp.exp(logit - _lane_tile(m_new, logit.shape[1] // LANES))
        if mult is not None:
            p = p * mult
        l_sc[...] = alpha * l_sc[...] + jnp.sum(p, axis=-1, keepdims=True)
        acc_sc[...] = _lane_tile(alpha, width // LANES) * acc_sc[...] + pv(p.astype(BF16))
        m_sc[...] = m_new

    @pl.when(s == 0)
    def _():
        acc_sc[...] = jnp.zeros_like(acc_sc)
        l_sc[...] = jnp.zeros_like(l_sc)
        m_sc[...] = jnp.full_like(m_sc, NEG)
        kn_sc[...] = jnp.zeros_like(kn_sc)
        vn_sc[...] = jnp.zeros_like(vn_sc)
        kn_sc[0:nq, :] = kn_ref[...].astype(BF16)
        vn_sc[0:nq, :] = vn_ref[...].astype(BF16)
        q = q_ref[...]
        for h in range(n_heads):
            qbd_sc[h * nq:(h + 1) * nq, :] = jnp.where(col_head == h, q * SCALE, 0.0).astype(BF16)
        causal = lane <= row_q
        scores = lax.dot_general(qbd_sc[...], kn_sc[...], (((1,), (1,)), ((), ())), preferred_element_type=F32)
        pv_new = lambda p: jnp.dot(p, vn_sc[...], preferred_element_type=F32)
        if mode == "moba":
            nb = group // pages_per_block
            blk_lane = lax.broadcasted_iota(jnp.int32, (width, LANES), 1)
            means = jnp.zeros((width, LANES), F32)
            for blk in range(nb):
                total = sum(kp[blk * pages_per_block + g][0] for g in range(pages_per_block))
                mean = jnp.sum(total.reshape(width, page), axis=-1, keepdims=True) * (1.0 / MOBA_BLOCK)
                means = jnp.where(blk_lane == blk, mean, means)
            q_bd = jnp.concatenate([jnp.where(col_head == h, q, 0.0) for h in range(n_heads)], axis=0)
            gate = jnp.dot(q_bd, means, preferred_element_type=F32, precision=HIGHEST)
            sel_sc[...] = _top_blocks(gate, float(nb), float(nb), LANES)
            attend(scores, pv_new, biasn_ref[...], causal, None)
        elif mode == "fox":
            lfn = lfn_ref[...]
            tri = (lax.broadcasted_iota(jnp.int32, (nq, nq), 1)
                   <= lax.broadcasted_iota(jnp.int32, (nq, nq), 0)).astype(F32)
            newcum = jnp.dot(tri, lfn, preferred_element_type=F32, precision=HIGHEST)
            eye = lane[:nq] == row_q[:nq]
            ncb, nrow = [], []
            for h in range(n_heads):
                col = jnp.broadcast_to(newcum[:, h:h + 1], (nq, LANES))
                ncb.append(col)
                nrow.append(jnp.broadcast_to(jnp.sum(jnp.where(eye, col, 0.0), axis=0, keepdims=True), (nq, LANES)))
            ncb = jnp.concatenate(ncb, axis=0)
            ncb_sc[...] = ncb
            run_sc[...] = jnp.zeros_like(run_sc)
            attend(scores, pv_new, ncb - jnp.concatenate(nrow, axis=0), causal, None)
        else:
            multn = multn_ref[...]
            attend(scores, pv_new, biasn_ref[...], multn > 0.0, multn)

    if paged:
        k_t = jnp.concatenate([kp[g][0].reshape(width, page) for g in range(group)], axis=1).astype(BF16)
        v_t = jnp.concatenate([vp[g][0].reshape(width, page) for g in range(group)], axis=1).astype(BF16)
    else:
        k_t = kp[0][0].reshape(width, group * page).astype(BF16)
        v_t = vp[0][0].reshape(width, group * page).astype(BF16)
    scores = jnp.dot(qbd_sc[...], k_t, preferred_element_type=F32)
    pv = lambda p: lax.dot_general(p, v_t, (((1,), (1,)), ((), ())), preferred_element_type=F32)
    if mode == "moba":
        hidden = []
        for g in range(group):
            blk = (s * group + g) // pages_per_block
            vis = jnp.max(jnp.where(lane == blk, sel_sc[...], 0.0), axis=-1, keepdims=True)
            hidden.append(jnp.broadcast_to(jnp.where(vis > 0.5, 0.0, NEG), (rows, page)))
        attend(scores, pv, bias_ref[0] + jnp.concatenate(hidden, axis=1), None, None)
    elif mode == "fox":
        lft = jnp.concatenate([lfp[g][0] for g in range(group)], axis=0)
        upper = (lax.broadcasted_iota(jnp.int32, (page, page), 0)
                 > lax.broadcasted_iota(jnp.int32, (page, page), 1)).astype(F32)
        inside = jnp.dot(lft, upper, preferred_element_type=F32, precision=HIGHEST)
        totals = jnp.sum(lft, axis=-1, keepdims=True)
        run = run_sc[...]
        pieces = []
        for g in range(group):
            after = inside[g * n_heads:(g + 1) * n_heads] + run
            pieces.append(jnp.concatenate(
                [jnp.broadcast_to(after[h:h + 1, :], (nq, page)) for h in range(n_heads)], axis=0))
            run = run + totals[g * n_heads:(g + 1) * n_heads]
        run_sc[...] = run
        attend(scores, pv, _lane_tile(ncb_sc[...], group) + jnp.concatenate(pieces, axis=1), None, None)
    else:
        mult = mult_ref[0]
        attend(scores, pv, bias_ref[0], mult > 0.0, mult)

    @pl.when(s == n_steps - 1)
    def _():
        out = acc_sc[...] / _lane_tile(l_sc[...], width // LANES)
        o_ref[...] = sum(jnp.where(col_head == h, out[h * nq:(h + 1) * nq, :], 0.0) for h in range(n_heads))


def _multiplicity(d):
    m = np.zeros(d.shape, np.float32)
    for window, dil in DILATED_BRANCHES:
        m += ((d >= 0) & (d <= window) & (d % dil == 0)).astype(np.float32)
    return m


def _sample_tables(past_len, nq, n_heads, n_steps, span):
    d = (past_len + np.arange(nq)[None, :, None]) - (np.arange(n_steps)[:, None, None] * span
                                                     + np.arange(span)[None, None, :])
    dn = np.arange(nq)[:, None] - np.arange(LANES)[None, :]
    slopes = np.repeat(np.asarray(_alibi_slopes(n_heads), np.float64), nq)[None, :, None]
    tile = lambda a: np.tile(a, (1, n_heads, 1))
    d, dn = tile(d), tile(dn[None])
    return ((-slopes * d).astype(np.float32), (-slopes * dn)[0].astype(np.float32),
            _multiplicity(d), _multiplicity(dn)[0])


def sample_attention(p, mode, row0, db, nq, n_heads, q_blk, k_blk, v_blk, k_pool_t, v_pool_t, page_table, extra=()):
    width = n_heads * HEAD_DIM
    rows = n_heads * nq
    reverse = mode == "fox"
    paged = page_table is not None
    page = k_pool_t.shape[3] if paged else LANES
    assert page == LANES
    n_pages = page_table.shape[1] if paged else k_pool_t.shape[3] // page
    group = min(n_pages, SAMPLE_PAGED_GROUP if paged else SAMPLE_SPAN_GROUP)
    pt = page_table.reshape(-1) if paged else jnp.zeros((1,), jnp.int32)
    n_steps = n_pages // group
    span = group * page
    rb0 = row0 // nq

    def pg(s, g):
        j = s * group + g
        return (n_pages - 1 - j) if reverse else j

    def pool_spec(g):
        return pl.BlockSpec((1, n_heads, HEAD_DIM, page), lambda b, s, t: (t[b * n_pages + pg(s, g)], 0, 0, 0))

    if paged:
        pool_specs = [pool_spec(g) for g in range(group)]
        pools = lambda a: [a] * group
    else:
        pool_specs = [pl.BlockSpec((1, n_heads, HEAD_DIM, span), lambda b, s, t: (b, 0, 0, s))]
        pools = lambda a: [a]
    bias, biasn, mult, multn = [jnp.asarray(a) for a in _sample_tables(n_pages * page, nq, n_heads, n_steps, span)]
    table_spec = pl.BlockSpec((1, rows, span), lambda b, s, t: (s, 0, 0))
    new_spec = pl.BlockSpec((rows, LANES), lambda b, s, t: (0, 0))
    in_specs = [pl.BlockSpec((nq, width), lambda b, s, t: (rb0 + b, q_blk)),
                pl.BlockSpec((nq, width), lambda b, s, t: (rb0 + b, k_blk)),
                pl.BlockSpec((nq, width), lambda b, s, t: (rb0 + b, v_blk))]
    in_specs += pool_specs * 2
    args = [p, p, p] + pools(k_pool_t) + pools(v_pool_t)
    scratch = [pltpu.VMEM((rows, width), BF16), pltpu.VMEM((LANES, width), BF16), pltpu.VMEM((LANES, width), BF16),
               pltpu.VMEM((rows, width), F32), pltpu.VMEM((rows, LANES), F32), pltpu.VMEM((rows, LANES), F32)]
    if mode == "moba":
        assert n_steps == 1
        in_specs += [table_spec, new_spec]
        args += [bias, biasn]
        scratch.append(pltpu.VMEM((rows, LANES), F32))
    elif mode == "fox":
        lf_pool_t, logf = extra
        in_specs += [pl.BlockSpec((1, n_heads, page),
                                  functools.partial(lambda b, s, t, g: (t[b * n_pages + pg(s, g)], 0, 0), g=g))
                     for g in range(group)]
        in_specs.append(pl.BlockSpec((nq, LANES), lambda b, s, t: (rb0 + b, 0)))
        args += [lf_pool_t] * group + [logf]
        scratch += [pltpu.VMEM((rows, LANES), F32), pltpu.VMEM((n_heads, LANES), F32)]
    else:
        in_specs += [table_spec, new_spec, table_spec, new_spec]
        args += [bias, biasn, mult, multn]
    return pl.pallas_call(
        functools.partial(_sample_body, mode=mode, n_heads=n_heads, n_steps=n_steps, group=group, page=page,
                          pages_per_block=MOBA_BLOCK // page, paged=paged),
        out_shape=jax.ShapeDtypeStruct((db * nq, width), F32),
        grid_spec=pltpu.PrefetchScalarGridSpec(
            num_scalar_prefetch=1,
            grid=(db, n_steps),
            in_specs=in_specs,
            out_specs=pl.BlockSpec((nq, width), lambda b, s, t: (b, 0)),
            scratch_shapes=scratch),
        compiler_params=_cparams("parallel", "arbitrary"),
        name="sample_attention_" + mode,
    )(pt, *args)


RANK_NONE = 127.0


def _top_rows(x, k, with_rank=False):
    vals = []
    rank = jnp.full(x.shape, RANK_NONE, F32)
    for r in range(k):
        m = jnp.max(x, axis=0, keepdims=True)
        vals.append(m)
        hit = x == m
        if with_rank:
            rank = jnp.where(hit, float(r), rank)
        x = jnp.where(hit, -jnp.inf, x)
    return (vals, rank) if with_rank else vals


def _bf16_twice(x):
    bits = lax.bitcast_convert_type(x.astype(BF16).astype(F32), jnp.uint32)
    return bits | (bits >> 16)


def _peer_scores_body(ht_ref, wq_ref, sk_ref, n1_ref, r2_ref, e1_ref, e2_ref):
    half = PEER_KEYS
    k1 = PEER_TOPK + 1
    qt = jnp.dot(wq_ref[...], ht_ref[...], preferred_element_type=F32)
    s1 = jnp.dot(sk_ref[0, 0], qt[:half], preferred_element_type=F32, precision=HIGHEST)
    s2 = jnp.dot(sk_ref[0, 1], qt[half:], preferred_element_type=F32, precision=HIGHEST)
    t1 = _top_rows(s1, k1)
    t2, rank2 = _top_rows(s2, k1, with_rank=True)
    cand = [t1[i] + t2[j] for i in range(k1) for j in range(k1) if (i + 1) * (j + 1) <= k1]
    pad = -len(cand) % SUBLANES
    cand = jnp.concatenate(cand + [jnp.full_like(t1[0], -jnp.inf)] * pad, axis=0)
    top = _top_rows(cand, k1)
    z = sum(jnp.exp(v - top[0]) for v in top[:PEER_TOPK])
    tau = 0.5 * (top[PEER_TOPK - 1] + top[PEER_TOPK])
    need = tau - s1
    n1 = jnp.zeros_like(s1)
    for j in range(k1):
        n1 = jnp.where(t2[j] >= need, float(j + 1), n1)
    n1_ref[0] = _bf16_twice(n1)
    r2_ref[0] = pltpu.bitcast(rank2.astype(BF16), jnp.uint32)
    e1_ref[0] = _bf16_twice(jnp.exp(s1 - t1[0]) / z)
    e2_ref[0] = pltpu.bitcast(jnp.exp(s2 - t2[0]).astype(BF16), jnp.uint32)


def peer_scores(ht, wq_t, subkeys):
    d, t = ht.shape
    tt = PEER_TOKEN_TILE
    nh = PEER_HEADS
    dq = wq_t.shape[0] // nh
    big = jax.ShapeDtypeStruct((nh, PEER_KEYS, t), jnp.uint32)
    small = jax.ShapeDtypeStruct((nh, PEER_KEYS // 2, t), jnp.uint32)
    bspec = pl.BlockSpec((1, PEER_KEYS, tt), lambda i, h: (h, 0, i))
    pspec = pl.BlockSpec((1, PEER_KEYS // 2, tt), lambda i, h: (h, 0, i))
    return pl.pallas_call(
        _peer_scores_body,
        out_shape=(big, small, big, small),
        grid=(t // tt, nh),
        in_specs=[pl.BlockSpec((d, tt), lambda i, h: (0, i)),
                  pl.BlockSpec((dq, d), lambda i, h: (h, 0)),
                  pl.BlockSpec((1,) + subkeys.shape[1:], lambda i, h: (h, 0, 0, 0))],
        out_specs=(bspec, pspec, bspec, pspec),
        compiler_params=_cparams("parallel", "arbitrary"),
        name="peer_scores",
    )(ht, wq_t, subkeys)


def _gelu(x):
    return 0.5 * x * (1.0 + lax.erf(x * (1.0 / math.sqrt(2.0))))


def _peer_experts_body(ht_ref, u_ref, vt_ref, n1_ref, r2_ref, e1_ref, e2_ref, x_ref, g_ref,
                       xn_ref, hn_ref, acc_sc, act_sc, w_sc):
    j = pl.program_id(1)
    te = u_ref.shape[0]
    tt = ht_ref.shape[1]
    groups = te // PEER_KEYS

    @pl.when(j == 0)
    def _():
        acc_sc[...] = jnp.zeros_like(acc_sc)

    act_sc[...] = _gelu(jnp.dot(u_ref[...], ht_ref[...], preferred_element_type=F32)).astype(BF16)
    n_tc = tt // LANES
    packed_rows = (PEER_KEYS // 2, LANES)
    for al in range(groups):
        rows = slice(al * PEER_KEYS, (al + 1) * PEER_KEYS)
        coef = [jnp.zeros((PEER_KEYS, LANES), BF16) for _ in range(n_tc)]
        for h in range(PEER_HEADS):
            n1_row = n1_ref[h, al:al + 1, :]
            e1_row = e1_ref[h, al:al + 1, :]
            for tc in range(n_tc):
                cols = slice(tc * LANES, (tc + 1) * LANES)
                n1 = pltpu.bitcast(jnp.broadcast_to(n1_row[:, cols], packed_rows), BF16)
                e1 = pltpu.bitcast(jnp.broadcast_to(e1_row[:, cols], packed_rows), BF16)
                hit = pltpu.bitcast(r2_ref[h, :, cols], BF16) < n1
                e2 = pltpu.bitcast(e2_ref[h, :, cols], BF16)
                coef[tc] = coef[tc] + jnp.where(hit, e2, jnp.zeros((), BF16)) * e1
        for tc in range(n_tc):
            cols = slice(tc * LANES, (tc + 1) * LANES)
            w_sc[rows, cols] = coef[tc] * act_sc[rows, cols]
    acc_sc[...] += jnp.dot(vt_ref[...], w_sc[...], preferred_element_type=F32)

    @pl.when(j == pl.num_programs(1) - 1)
    def _():
        xn = x_ref[...] + acc_sc[...].T
        xn_ref[...] = xn
        hn_ref[...] = _rms(xn, g_ref[...]).astype(hn_ref.dtype)


def peer_experts(x, ht, u, v_t, scores, g_next, norm_dtype):
    t, d = x.shape
    tt = PEER_TOKEN_TILE
    te = PEER_EXPERT_TILE
    n_exp = u.shape[0]
    sspec = pl.BlockSpec((PEER_HEADS, te // PEER_KEYS, tt), lambda i, j: (0, j, i))
    pspec = pl.BlockSpec((PEER_HEADS, PEER_KEYS // 2, tt), lambda i, j: (0, 0, i))
    return pl.pallas_call(
        _peer_experts_body,
        out_shape=(jax.ShapeDtypeStruct((t, d), F32), jax.ShapeDtypeStruct((t, d), norm_dtype)),
        grid=(t // tt, n_exp // te),
        in_specs=[pl.BlockSpec((d, tt), lambda i, j: (0, i)),
                  pl.BlockSpec((te, d), lambda i, j: (j, 0)),
                  pl.BlockSpec((d, te), lambda i, j: (0, j)),
                  sspec, pspec, sspec, pspec,
                  pl.BlockSpec((tt, d), lambda i, j: (i, 0)),
                  pl.BlockSpec((1, d), lambda i, j: (0, 0))],
        out_specs=(pl.BlockSpec((tt, d), lambda i, j: (i, 0)),
                   pl.BlockSpec((tt, d), lambda i, j: (i, 0))),
        scratch_shapes=[pltpu.VMEM((d, tt), F32), pltpu.VMEM((te, tt), BF16), pltpu.VMEM((te, tt), BF16)],
        compiler_params=_cparams("parallel", "arbitrary"),
        name="peer_experts",
    )(ht, u, v_t, *scores, x, g_next.reshape(1, d))


def _head_major_body(cb_ref, x_ref, o_ref):
    n_heads = o_ref.shape[2]
    o_ref[0, 0] = x_ref[...].T.reshape(n_heads, HEAD_DIM, x_ref.shape[0])


def head_major(p, col_blocks, batch, seq_len, row_start, rows, n_heads):
    width = n_heads * HEAD_DIM
    t = ROW_TILE
    assert rows % t == 0 and row_start % t == 0 and seq_len % t == 0
    return pl.pallas_call(
        _head_major_body,
        out_shape=jax.ShapeDtypeStruct((len(col_blocks), batch, n_heads, HEAD_DIM, rows), F32),
        grid_spec=pltpu.PrefetchScalarGridSpec(
            num_scalar_prefetch=1,
            grid=(len(col_blocks), batch, rows // t),
            in_specs=[pl.BlockSpec((t, width), lambda g, b, i, cb: ((b * seq_len + row_start) // t + i, cb[g]))],
            out_specs=pl.BlockSpec((1, 1, n_heads, HEAD_DIM, t), lambda g, b, i, cb: (g, b, 0, 0, i))),
        compiler_params=_cparams("parallel", "parallel", "parallel"),
        name="head_major",
    )(jnp.asarray(col_blocks, jnp.int32), p)


def _heads(a, n):
    return a.reshape(a.shape[:-1] + (n, HEAD_DIM))


def kernel(x_prompt, x_sample, cache_moba_k, cache_moba_v, cache_fox_k, cache_fox_v, cache_fox_logf,
           state_swa_k, state_swa_v, page_table, norm_gain, final_gain, w_in_even, b_forget, w_out_even,
           w_in_odd, w_out_odd, peer_w_query, peer_subkeys, peer_u, peer_v):
    b, s, d = x_prompt.shape
    db, ds, _ = x_sample.shape
    tp = b * s
    depth = norm_gain.shape[0]
    n_pool, page = cache_moba_k.shape[1:3]
    past_len = page_table.shape[1] * page
    win_buf = state_swa_k.shape[2]
    keep_p = min(WIN_MAX, s)
    assert win_buf == past_len and past_len % MOBA_BLOCK == 0 and MOBA_BLOCK % page == 0
    assert s % ATT_TILE == 0 and (tp + db * ds) % ROW_TILE == 0 and tp % ds == 0

    x = jnp.concatenate([x_prompt.reshape(tp, d), x_sample.reshape(db * ds, d)], axis=0)
    h = rmsnorm_rows(x, norm_gain[0, 0], BF16)
    new = {}
    for layer in range(depth):
        li = layer // 2
        if layer % 2 == 0:
            n_in = w_in_even.shape[2]
            n_pad = -n_in % (5 * LANES)
            w_in = jnp.pad(w_in_even[li], ((0, 0), (0, n_pad))).astype(BF16)
            p = matmul(h, w_in, (n_in + n_pad) // 5)
            logf, fox_kx = fox_gate(p, 3 * (D_A + D_B) // LANES, b_forget[li], s)
            km = moba_block_means(p, 1, b, s)
            oa_p = prompt_attention(p, "moba", b, s, N_HEADS_A, 0, D_A // LANES, 2 * D_A // LANES, (km,))
            fb = 3 * D_A // LANES
            ob_p = prompt_attention(p, "fox", b, s, N_HEADS_B, fb, fb + D_B // LANES, fb + 2 * D_B // LANES,
                                    (fox_kx,))
            pool = lambda c: jnp.transpose(c[li], (0, 2, 3, 1))
            ka_pool, va_pool = pool(cache_moba_k), pool(cache_moba_v)
            kb_pool, vb_pool = pool(cache_fox_k), pool(cache_fox_v)
            oa_s = sample_attention(p, "moba", tp, db, ds, N_HEADS_A, 0, 1, 2, ka_pool, va_pool, page_table)
            lf_pool_t = jnp.swapaxes(cache_fox_logf[li], 1, 2)
            ob_s = sample_attention(p, "fox", tp, db, ds, N_HEADS_B, 3, 4, 5, kb_pool, vb_pool, page_table,
                                    (lf_pool_t, logf))
            w_out = w_out_even[li].astype(BF16)
            pairs = [(jnp.concatenate([oa_p, oa_s.astype(BF16)], axis=0), w_out[:D_A]),
                     (jnp.concatenate([ob_p, ob_s.astype(BF16)], axis=0), w_out[D_A:])]
            assert N_HEADS_A == N_HEADS_B
            names = ("mk", "mv", "fk", "fv")
            col_blocks = (1, 2, 4, 5)
            stored = head_major(p, col_blocks, b, s, 0, s, N_HEADS_A)
            for g, name in enumerate(names):
                cols = p[tp:, col_blocks[g] * D_A:(col_blocks[g] + 1) * D_A]
                new.setdefault("p_" + name, []).append(jnp.transpose(stored[g], (0, 3, 1, 2)))
                new.setdefault("s_" + name, []).append(_heads(cols.reshape(db, ds, -1), N_HEADS_A))
            new.setdefault("p_fl", []).append(logf[:tp, :N_HEADS_B].reshape(b, s, N_HEADS_B))
            new.setdefault("s_fl", []).append(logf[tp:, :N_HEADS_B].reshape(db, ds, N_HEADS_B))
        else:
            p = matmul(h, w_in_odd[li].astype(BF16), D_C // 2)
            nb = D_C // LANES
            oc_p = prompt_attention(p, "dil", b, s, N_HEADS_C, 0, nb, 2 * nb)
            kc_pool = jnp.transpose(state_swa_k[li], (0, 2, 3, 1))
            vc_pool = jnp.transpose(state_swa_v[li], (0, 2, 3, 1))
            oc_s = sample_attention(p, "dil", tp, db, ds, N_HEADS_C, 0, 1, 2, kc_pool, vc_pool, None)
            pairs = [(jnp.concatenate([oc_p, oc_s.astype(BF16)], axis=0), w_out_odd[li].astype(BF16))]
            stored = head_major(p, (1, 2), b, s, s - keep_p, keep_p, N_HEADS_C)
            for g, name in enumerate(("sk", "sv")):
                cols = p[tp:, (g + 1) * D_C:(g + 2) * D_C]
                new.setdefault("p_" + name, []).append(jnp.transpose(stored[g], (0, 3, 1, 2)))
                new.setdefault("s_" + name, []).append(_heads(cols.reshape(db, ds, -1), N_HEADS_C))
        x, _, h_t = resid_norm(x, pairs, norm_gain[layer, 1])
        scores = peer_scores(h_t, peer_w_query[layer].T.astype(BF16), peer_subkeys[layer])
        last = layer == depth - 1
        g_next = final_gain if last else norm_gain[layer + 1, 0]
        x, h = peer_experts(x, h_t, peer_u[layer].astype(BF16), peer_v[layer].T.astype(BF16), scores,
                            g_next, F32 if last else BF16)
    y = h
    order = ["p_mk", "p_mv", "p_fk", "p_fv", "p_fl", "p_sk", "p_sv",
             "s_mk", "s_mv", "s_fk", "s_fv", "s_fl", "s_sk", "s_sv"]
    return (y[:tp].reshape(b, s, d), y[tp:].reshape(db, ds, d)) + tuple(jnp.stack(new[k]) for k in order)
```
